```python
import math
import jax
import jax.numpy as jnp
from jax import lax
import numpy as np

D_MODEL = 4096
BATCH = 1
SEQ = 16384
DEPTH = 4

N_MIXERS = 3
ATTN_BLOCK = 128
MASK_VALUE = -1e30
NORM_EPS = 1e-6
SWA_Q_HEADS = 64
SWA_KV_HEADS = 8
SWA_HEAD_DIM = D_MODEL // SWA_Q_HEADS
SWA_GROUP = SWA_Q_HEADS // SWA_KV_HEADS
WINDOW = 128
ROPE_THETA = 500000.0
ROPE_DIM = SWA_HEAD_DIM // 4
FOX_HEADS = 32
FOX_HEAD_DIM = D_MODEL // FOX_HEADS
S5_GROUP = 16
S5_GROUPS = D_MODEL // S5_GROUP
S5_STATE = 64
S5_CHUNK = 128
S5_DT_MIN = 1e-3
S5_DT_MAX = 1e-1
S5_MAX_RE = -1e-4
D_FF = 256 * ((8 * D_MODEL // 3 + 255) // 256)
CONV_WIDTH = 3

kernel_name = 'hybrid_swa_fox_s5_convffn'


def _rms_norm(x, g):
    xf = x.astype(jnp.float32)
    y = xf * lax.rsqrt(jnp.mean(xf * xf, axis=-1, keepdims=True) + NORM_EPS)
    return (y * g.astype(jnp.float32)).astype(x.dtype)


def _partial_rope(t, pos):
    half = ROPE_DIM // 2
    inv_freq = jnp.exp(-math.log(ROPE_THETA) * jnp.arange(half, dtype=jnp.float32) * (2.0 / ROPE_DIM))
    ang = pos.astype(jnp.float32)[:, None] * inv_freq[None, :]
    cos = jnp.cos(ang)[None, :, None, :]
    sin = jnp.sin(ang)[None, :, None, :]
    tr = t[..., :ROPE_DIM].astype(jnp.float32)
    t1, t2 = tr[..., :half], tr[..., half:]
    rot = jnp.concatenate([t1 * cos - t2 * sin, t2 * cos + t1 * sin], axis=-1).astype(t.dtype)
    return jnp.concatenate([rot, t[..., ROPE_DIM:]], axis=-1)


def _swa_mixer(h, w_qkv, b_qkv, sinks, w_o, b_o):
    b, l, _ = h.shape
    nb = l // ATTN_BLOCK
    qw = SWA_Q_HEADS * SWA_HEAD_DIM
    kvw = SWA_KV_HEADS * SWA_HEAD_DIM
    qkv = h @ w_qkv + b_qkv
    q = qkv[..., :qw].reshape(b, l, SWA_Q_HEADS, SWA_HEAD_DIM)
    k = qkv[..., qw:qw + kvw].reshape(b, l, SWA_KV_HEADS, SWA_HEAD_DIM)
    v = qkv[..., qw + kvw:].reshape(b, l, SWA_KV_HEADS, SWA_HEAD_DIM)
    pos = jnp.arange(l)
    q = _partial_rope(q, pos)
    k = _partial_rope(k, pos)
    qb = q.reshape(b, nb, ATTN_BLOCK, SWA_KV_HEADS, SWA_GROUP, SWA_HEAD_DIM)

    def band(t):
        tb = t.reshape(b, nb, ATTN_BLOCK, SWA_KV_HEADS, SWA_HEAD_DIM)
        prev = jnp.pad(tb[:, :-1], ((0, 0), (1, 0), (0, 0), (0, 0), (0, 0)))
        return jnp.concatenate([prev, tb], axis=2)

    kb, vb = band(k), band(v)
    s = jnp.einsum('bnqhgd,bnkhd->bnhgqk', qb, kb).astype(jnp.float32) * (SWA_HEAD_DIM ** -0.5)
    qi = jnp.arange(ATTN_BLOCK)[:, None]
    kj = jnp.arange(2 * ATTN_BLOCK)[None, :]
    rel = qi + ATTN_BLOCK - kj
    kpos = (jnp.arange(nb)[:, None, None] - 1) * ATTN_BLOCK + kj[None]
    valid = (rel >= 0) & (rel < WINDOW) & (kpos >= 0)
    s = jnp.where(valid[None, :, None, None], s, MASK_VALUE)
    sink = sinks.astype(jnp.float32).reshape(1, 1, SWA_KV_HEADS, SWA_GROUP, 1, 1)
    sink = jnp.broadcast_to(sink, s.shape[:-1] + (1,))
    p = jax.nn.softmax(jnp.concatenate([s, sink], axis=-1), axis=-1)[..., :-1]
    o = jnp.einsum('bnhgqk,bnkhd->bnqhgd', p.astype(vb.dtype), vb).reshape(b, l, qw)
    return o @ w_o + b_o


def _fox_mixer(h, w_qkvf, b_f, w_o):
    b, l, _ = h.shape
    nb = l // ATTN_BLOCK
    hw = FOX_HEADS * FOX_HEAD_DIM
    proj = h @ w_qkvf

    def heads(t):
        return t.reshape(b, l, FOX_HEADS, FOX_HEAD_DIM).transpose(0, 2, 1, 3)

    q = heads(proj[..., :hw])
    k = heads(proj[..., hw:2 * hw])
    v = heads(proj[..., 2 * hw:3 * hw])
    log_f = jax.nn.log_sigmoid(proj[..., 3 * hw:].astype(jnp.float32) + b_f.astype(jnp.float32))
    cum = jnp.cumsum(log_f, axis=1).transpose(0, 2, 1)
    q_blocks = q.reshape(b, FOX_HEADS, nb, ATTN_BLOCK, FOX_HEAD_DIM).transpose(2, 0, 1, 3, 4)
    cum_blocks = cum.reshape(b, FOX_HEADS, nb, ATTN_BLOCK).transpose(2, 0, 1, 3)
    kpos = jnp.arange(l)
    scale = FOX_HEAD_DIM ** -0.5

    def one_block(args):
        qb, cq, n = args
        s = jnp.einsum('bhqd,bhkd->bhqk', qb, k).astype(jnp.float32) * scale
        s = s + cq[..., :, None] - cum[:, :, None, :]
        qpos = n * ATTN_BLOCK + jnp.arange(ATTN_BLOCK)
        s = jnp.where(kpos[None, :] <= qpos[:, None], s, MASK_VALUE)
        p = jax.nn.softmax(s, axis=-1)
        return jnp.einsum('bhqk,bhkd->bhqd', p.astype(v.dtype), v)

    o = lax.map(one_block, (q_blocks, cum_blocks, jnp.arange(nb)))
    o = o.transpose(1, 0, 3, 2, 4).reshape(b, l, hw)
    return o @ w_o


def _ssm_combine(e1, e2):
    a1, b1 = e1
    a2, b2 = e2
    return a1 * a2, a2 * b1 + b2


def _s5_mixer(h, lam_re, lam_im, log_dt, b_re, b_im, c_re, c_im, d, w_glu, b_glu):
    b, l, _ = h.shape
    f32 = jnp.float32
    nc = l // S5_CHUNK
    lam = lax.complex(jnp.minimum(lam_re.astype(f32), S5_MAX_RE), lam_im.astype(f32))
    dt = jnp.exp(log_dt.astype(f32))[:, None]
    lam_bar = jnp.exp(lam * dt)
    bmat = lax.complex(b_re.astype(f32), b_im.astype(f32))
    b_bar = ((lam_bar - 1.0) / lam)[..., None] * bmat
    cmat = lax.complex(c_re.astype(f32), c_im.astype(f32))
    u = h.astype(f32).reshape(b, nc, S5_CHUNK, S5_GROUPS, S5_GROUP).transpose(1, 0, 2, 3, 4)
    a_elems = jnp.broadcast_to(lam_bar, (b, S5_CHUNK, S5_GROUPS, S5_STATE))

    def chunk_step(state, u_c):
        bu = jnp.einsum('btgh,gph->btgp', u_c.astype(jnp.complex64), b_bar)
        bu = bu.at[:, 0].add(lam_bar * state)
        _, xs = lax.associative_scan(_ssm_combine, (a_elems, bu), axis=1)
        y = jnp.einsum('btgp,ghp->btgh', xs, cmat).real
        return xs[:, -1], y

    init = jnp.zeros((b, S5_GROUPS, S5_STATE), jnp.complex64)
    _, y = lax.scan(chunk_step, init, u)
    y = y.transpose(1, 0, 2, 3, 4).reshape(b, l, D_MODEL) + d.astype(f32) * h.astype(f32)
    g = jax.nn.gelu(y).astype(h.dtype)
    z = g @ w_glu + b_glu
    return z[..., :D_MODEL] * jax.nn.sigmoid(z[..., D_MODEL:])


def _conv_ffn(h, w_in, conv_w, conv_b, w_out):
    l = h.shape[1]
    z = h @ w_in
    zp = jnp.pad(z, ((0, 0), (CONV_WIDTH - 1, 0), (0, 0)))
    zc = conv_b + conv_w[CONV_WIDTH - 1] * z
    for j in range(CONV_WIDTH - 1):
        zc = zc + conv_w[j] * zp[:, j:j + l]
    gate, up = zc[..., :D_FF], zc[..., D_FF:]
    return (jax.nn.gelu(gate, approximate=True) * up) @ w_out


def _normal(key, shape, scale):
    return jax.random.normal(key, shape, jnp.float32) * scale


def _gain(key):
    return 1.0 + _normal(key, (D_MODEL,), 0.1)


def _swa_params(key):
    k = jax.random.split(key, 5)
    qw = SWA_Q_HEADS * SWA_HEAD_DIM
    qkv_w = qw + 2 * SWA_KV_HEADS * SWA_HEAD_DIM
    return [('swa_w_qkv', _normal(k[0], (D_MODEL, qkv_w), D_MODEL ** -0.5)),
            ('swa_b_qkv', _normal(k[1], (qkv_w,), 0.01)),
            ('swa_sinks', _normal(k[2], (SWA_Q_HEADS,), 1.0)),
            ('swa_w_o', _normal(k[3], (qw, D_MODEL), qw ** -0.5)),
            ('swa_b_o', _normal(k[4], (D_MODEL,), 0.01))]


def _fox_params(key):
    k = jax.random.split(key, 3)
    hw = FOX_HEADS * FOX_HEAD_DIM
    return [('fox_w_qkvf', _normal(k[0], (D_MODEL, 3 * hw + FOX_HEADS), D_MODEL ** -0.5)),
            ('fox_b_f', jax.random.uniform(k[1], (FOX_HEADS,), jnp.float32, 1.0, 5.0)),
            ('fox_w_o', _normal(k[2], (hw, D_MODEL), hw ** -0.5))]


def _s5_params(key):
    k = jax.random.split(key, 10)
    n = jnp.arange(S5_STATE, dtype=jnp.float32)[None, :]
    return [('s5_lambda_re', -0.5 + _normal(k[0], (S5_GROUPS, S5_STATE), 0.01)),
            ('s5_lambda_im', math.pi * n + _normal(k[1], (S5_GROUPS, S5_STATE), 0.01)),
            ('s5_log_dt', jax.random.uniform(k[2], (S5_GROUPS,), jnp.float32, math.log(S5_DT_MIN), math.log(S5_DT_MAX))),
            ('s5_b_re', _normal(k[3], (S5_GROUPS, S5_STATE, S5_GROUP), (2 * S5_GROUP) ** -0.5)),
            ('s5_b_im', _normal(k[4], (S5_GROUPS, S5_STATE, S5_GROUP), (2 * S5_GROUP) ** -0.5)),
            ('s5_c_re', _normal(k[5], (S5_GROUPS, S5_GROUP, S5_STATE), S5_STATE ** -0.5)),
            ('s5_c_im', _normal(k[6], (S5_GROUPS, S5_GROUP, S5_STATE), S5_STATE ** -0.5)),
            ('s5_d', _normal(k[7], (D_MODEL,), 1.0)),
            ('s5_w_glu', _normal(k[8], (D_MODEL, 2 * D_MODEL), D_MODEL ** -0.5)),
            ('s5_b_glu', _normal(k[9], (2 * D_MODEL,), 0.01))]


def _ffn_params(key):
    k = jax.random.split(key, 4)
    return [('ffn_w_in', _normal(k[0], (D_MODEL, 2 * D_FF), D_MODEL ** -0.5)),
            ('ffn_conv_w', _normal(k[1], (CONV_WIDTH, 2 * D_FF), CONV_WIDTH ** -0.5)),
            ('ffn_conv_b', _normal(k[2], (2 * D_FF,), 0.01)),
            ('ffn_w_out', _normal(k[3], (D_FF, D_MODEL), D_FF ** -0.5))]


def setup_inputs(seed: int = 0) -> dict:
    key = jax.random.key(seed)
    keys = jax.random.split(key, DEPTH + 1)
    builders = (_swa_params, _fox_params, _s5_params)
    params = {'x': jax.random.normal(keys[0], (BATCH, SEQ, D_MODEL), jnp.float32)}
    for i in range(DEPTH):
        kl = jax.random.split(keys[i + 1], 6)
        p = f'l{i}_'
        params[p + 'mix_pre_g'] = _gain(kl[0])
        for name, arr in builders[i % N_MIXERS](kl[1]):
            params[p + name] = arr
        params[p + 'mix_post_g'] = _gain(kl[2])
        params[p + 'ffn_pre_g'] = _gain(kl[3])
        for name, arr in _ffn_params(kl[4]):
            params[p + name] = arr
        params[p + 'ffn_post_g'] = _gain(kl[5])
    return params


def reference(x,
              l0_mix_pre_g, l0_swa_w_qkv, l0_swa_b_qkv, l0_swa_sinks, l0_swa_w_o, l0_swa_b_o, l0_mix_post_g,
              l0_ffn_pre_g, l0_ffn_w_in, l0_ffn_conv_w, l0_ffn_conv_b, l0_ffn_w_out, l0_ffn_post_g,
              l1_mix_pre_g, l1_fox_w_qkvf, l1_fox_b_f, l1_fox_w_o, l1_mix_post_g,
              l1_ffn_pre_g, l1_ffn_w_in, l1_ffn_conv_w, l1_ffn_conv_b, l1_ffn_w_out, l1_ffn_post_g,
              l2_mix_pre_g, l2_s5_lambda_re, l2_s5_lambda_im, l2_s5_log_dt, l2_s5_b_re, l2_s5_b_im,
              l2_s5_c_re, l2_s5_c_im, l2_s5_d, l2_s5_w_glu, l2_s5_b_glu, l2_mix_post_g,
              l2_ffn_pre_g, l2_ffn_w_in, l2_ffn_conv_w, l2_ffn_conv_b, l2_ffn_w_out, l2_ffn_post_g,
              l3_mix_pre_g, l3_swa_w_qkv, l3_swa_b_qkv, l3_swa_sinks, l3_swa_w_o, l3_swa_b_o, l3_mix_post_g,
              l3_ffn_pre_g, l3_ffn_w_in, l3_ffn_conv_w, l3_ffn_conv_b, l3_ffn_w_out, l3_ffn_post_g):
    layers = (
        (l0_mix_pre_g, (l0_swa_w_qkv, l0_swa_b_qkv, l0_swa_sinks, l0_swa_w_o, l0_swa_b_o), l0_mix_post_g,
         l0_ffn_pre_g, (l0_ffn_w_in, l0_ffn_conv_w, l0_ffn_conv_b, l0_ffn_w_out), l0_ffn_post_g),
        (l1_mix_pre_g, (l1_fox_w_qkvf, l1_fox_b_f, l1_fox_w_o), l1_mix_post_g,
         l1_ffn_pre_g, (l1_ffn_w_in, l1_ffn_conv_w, l1_ffn_conv_b, l1_ffn_w_out), l1_ffn_post_g),
        (l2_mix_pre_g, (l2_s5_lambda_re, l2_s5_lambda_im, l2_s5_log_dt, l2_s5_b_re, l2_s5_b_im,
                        l2_s5_c_re, l2_s5_c_im, l2_s5_d, l2_s5_w_glu, l2_s5_b_glu), l2_mix_post_g,
         l2_ffn_pre_g, (l2_ffn_w_in, l2_ffn_conv_w, l2_ffn_conv_b, l2_ffn_w_out), l2_ffn_post_g),
        (l3_mix_pre_g, (l3_swa_w_qkv, l3_swa_b_qkv, l3_swa_sinks, l3_swa_w_o, l3_swa_b_o), l3_mix_post_g,
         l3_ffn_pre_g, (l3_ffn_w_in, l3_ffn_conv_w, l3_ffn_conv_b, l3_ffn_w_out), l3_ffn_post_g),
    )
    mixers = (_swa_mixer, _fox_mixer, _s5_mixer)
    h = x
    for i in range(DEPTH):
        mix_pre, mix_params, mix_post, ffn_pre, ffn_params, ffn_post = layers[i]
        h = h + _rms_norm(mixers[i % N_MIXERS](_rms_norm(h, mix_pre), *mix_params), mix_post)
        h = h + _rms_norm(_conv_ffn(_rms_norm(h, ffn_pre), *ffn_params), ffn_post)
    return h
```

```python
import functools
import math

import jax
import jax.numpy as jnp
from jax import lax
from jax.experimental import pallas as pl
from jax.experimental.pallas import tpu as pltpu

F32 = jnp.float32
BF16 = jnp.bfloat16

NORM_EPS = 1e-6
MASK_VALUE = -1e30
ATTN_BLOCK = 128
WINDOW = 128
ROPE_THETA = 500000.0
SWA_KV_HEADS_PER_Q = 8
SWA_HEAD_DIM = 64
FOX_HEAD_DIM = 128
S5_GROUP = 16
S5_STATE = 64
S5_MAX_RE = -1e-4
CONV_WIDTH = 3

LANES = 128
SUBLANES = 8
VMEM_LIMIT_BYTES = 56 * 1024 * 1024


def _params(sem, vmem=VMEM_LIMIT_BYTES):
    return pltpu.CompilerParams(dimension_semantics=sem, vmem_limit_bytes=vmem)


def _tile(dim, pref, mult):
    if dim <= pref:
        return dim
    t = (pref // mult) * mult
    while t > mult and dim % t:
        t -= mult
    assert dim % t == 0, (dim, pref, mult)
    return t


def _rms(x, g):
    ms = jnp.mean(x * x, axis=-1, keepdims=True)
    return x * lax.rsqrt(ms + NORM_EPS) * g


def _prenorm_kernel(x_ref, g_ref, o_ref):
    o_ref[...] = _rms(x_ref[...], g_ref[...]).astype(o_ref.dtype)


def _prenorm(x, g):
    m, d = x.shape
    tr = _tile(m, 256, SUBLANES)
    return pl.pallas_call(
        _prenorm_kernel,
        grid=(m // tr,),
        in_specs=[pl.BlockSpec((tr, d), lambda i: (i, 0)), pl.BlockSpec((1, d), lambda i: (0, 0))],
        out_specs=pl.BlockSpec((tr, d), lambda i: (i, 0)),
        out_shape=jax.ShapeDtypeStruct((m, d), BF16),
        compiler_params=_params(("parallel",)),
        name="prenorm",
    )(x, g.reshape(1, d))


def _post_pre_kernel(y_ref, h_ref, gp_ref, gn_ref, h_out_ref, hn_ref):
    h_new = h_ref[...] + _rms(y_ref[...].astype(F32), gp_ref[...])
    h_out_ref[...] = h_new
    hn_ref[...] = _rms(h_new, gn_ref[...]).astype(hn_ref.dtype)


def _post_kernel(y_ref, h_ref, gp_ref, h_out_ref):
    h_out_ref[...] = h_ref[...] + _rms(y_ref[...].astype(F32), gp_ref[...])


def _post_pre(y, h, g_post, g_next):
    m, d = h.shape
    tr = _tile(m, 256, SUBLANES)
    row = pl.BlockSpec((tr, d), lambda i: (i, 0))
    vec = pl.BlockSpec((1, d), lambda i: (0, 0))
    if g_next is None:
        return pl.pallas_call(
            _post_kernel, grid=(m // tr,), in_specs=[row, row, vec], out_specs=row,
            out_shape=jax.ShapeDtypeStruct((m, d), F32),
            compiler_params=_params(("parallel",)), name="post",
        )(y, h, g_post.reshape(1, d)), None
    return pl.pallas_call(
        _post_pre_kernel, grid=(m // tr,), in_specs=[row, row, vec, vec], out_specs=[row, row],
        out_shape=[jax.ShapeDtypeStruct((m, d), F32), jax.ShapeDtypeStruct((m, d), BF16)],
        compiler_params=_params(("parallel",)), name="post_pre",
    )(y, h, g_post.reshape(1, d), g_next.reshape(1, d))


def _mm_kernel(x_ref, w_ref, o_ref):
    o_ref[...] = jnp.dot(x_ref[...], w_ref[...], preferred_element_type=F32).astype(o_ref.dtype)


def _mm_bias_kernel(x_ref, w_ref, b_ref, o_ref):
    acc = jnp.dot(x_ref[...], w_ref[...], preferred_element_type=F32) + b_ref[...]
    o_ref[...] = acc.astype(o_ref.dtype)


def _mm(x, w, b, out_dtype, tm_pref=1024, tn_pref=1024, name="mm"):
    m, k = x.shape
    n = w.shape[1]
    tm = _tile(m, tm_pref, 16)
    tn = _tile(n, tn_pref, LANES)
    in_specs = [pl.BlockSpec((tm, k), lambda i, j: (i, 0)), pl.BlockSpec((k, tn), lambda i, j: (0, j))]
    args = [x, w]
    body = _mm_kernel
    if b is not None:
        in_specs.append(pl.BlockSpec((1, tn), lambda i, j: (0, j)))
        args.append(b.reshape(1, n).astype(F32))
        body = _mm_bias_kernel
    return pl.pallas_call(
        body, grid=(m // tm, n // tn), in_specs=in_specs,
        out_specs=pl.BlockSpec((tm, tn), lambda i, j: (i, j)),
        out_shape=jax.ShapeDtypeStruct((m, n), out_dtype),
        compiler_params=_params(("parallel", "arbitrary")), name=name,
    )(*args)


def _glu_kernel(x_ref, w1_ref, w2_ref, b1_ref, b2_ref, o_ref):
    x = x_ref[...]
    z1 = jnp.dot(x, w1_ref[...], preferred_element_type=F32) + b1_ref[...]
    z2 = jnp.dot(x, w2_ref[...], preferred_element_type=F32) + b2_ref[...]
    o_ref[...] = (z1 * jax.nn.sigmoid(z2)).astype(o_ref.dtype)


def _glu_mm(x, w, b, out_dtype):
    m, k = x.shape
    n = w.shape[1] // 2
    tm = _tile(m, 1024, 16)
    tn = _tile(n, 512, LANES)
    nj = n // tn
    b2 = b.reshape(1, 2 * n).astype(F32)
    return pl.pallas_call(
        _glu_kernel, grid=(m // tm, nj),
        in_specs=[pl.BlockSpec((tm, k), lambda i, j: (i, 0)),
                  pl.BlockSpec((k, tn), lambda i, j: (0, j)),
                  pl.BlockSpec((k, tn), lambda i, j: (0, j + nj)),
                  pl.BlockSpec((1, tn), lambda i, j: (0, j)),
                  pl.BlockSpec((1, tn), lambda i, j: (0, j + nj))],
        out_specs=pl.BlockSpec((tm, tn), lambda i, j: (i, j)),
        out_shape=jax.ShapeDtypeStruct((m, n), out_dtype),
        compiler_params=_params(("parallel", "arbitrary")), name="glu_mm",
    )(x, w, w, b2, b2)


def _ffn_in_kernel(x_ref, wg_ref, wu_ref, cwg_ref, cwu_ref, cbg_ref, cbu_ref, o_ref, carry_g, carry_u):
    i = pl.program_id(0)
    j = pl.program_id(1)
    x = x_ref[...]
    tm = x.shape[0]

    @pl.when(i == 0)
    def _():
        carry_g[j] = jnp.zeros(carry_g.shape[1:], F32)
        carry_u[j] = jnp.zeros(carry_u.shape[1:], F32)

    def conv(w_ref, cw_ref, cb_ref, carry_ref):
        z = jnp.dot(x, w_ref[...], preferred_element_type=F32)
        row = lax.broadcasted_iota(jnp.int32, z.shape, 0)
        prev = carry_ref[j]
        p_m1 = prev[SUBLANES - 1:SUBLANES, :]
        p_m2 = prev[SUBLANES - 2:SUBLANES - 1, :]
        z1 = jnp.where(row == 0, p_m1, pltpu.roll(z, 1, 0))
        z2 = jnp.where(row == 0, p_m2, jnp.where(row == 1, p_m1, pltpu.roll(z, 2, 0)))
        carry_ref[j] = z[tm - SUBLANES:, :]
        cw = cw_ref[...]
        return cb_ref[...] + cw[2:3, :] * z + cw[0:1, :] * z2 + cw[1:2, :] * z1

    gate = conv(wg_ref, cwg_ref, cbg_ref, carry_g)
    up = conv(wu_ref, cwu_ref, cbu_ref, carry_u)
    o_ref[...] = (jax.nn.gelu(gate, approximate=True) * up).astype(o_ref.dtype)


def _ffn_in(x, w_in, conv_w, conv_b):
    m, k = x.shape
    dff = w_in.shape[1] // 2
    tm = _tile(m, 1024, 16)
    tn = _tile(dff, 256, LANES)
    nj = dff // tn
    cb = conv_b.reshape(1, 2 * dff)
    return pl.pallas_call(
        _ffn_in_kernel, grid=(m // tm, nj),
        in_specs=[pl.BlockSpec((tm, k), lambda i, j: (i, 0)),
                  pl.BlockSpec((k, tn), lambda i, j: (0, j)),
                  pl.BlockSpec((k, tn), lambda i, j: (0, j + nj)),
                  pl.BlockSpec((CONV_WIDTH, tn), lambda i, j: (0, j)),
                  pl.BlockSpec((CONV_WIDTH, tn), lambda i, j: (0, j + nj)),
                  pl.BlockSpec((1, tn), lambda i, j: (0, j)),
                  pl.BlockSpec((1, tn), lambda i, j: (0, j + nj))],
        out_specs=pl.BlockSpec((tm, tn), lambda i, j: (i, j)),
        out_shape=jax.ShapeDtypeStruct((m, dff), BF16),
        scratch_shapes=[pltpu.VMEM((nj, SUBLANES, tn), F32), pltpu.VMEM((nj, SUBLANES, tn), F32)],
        compiler_params=_params(("arbitrary", "arbitrary")), name="ffn_in",
    )(x, w_in, w_in, conv_w, conv_w, cb, cb)


def _swa_kernel(sinks_ref, q_ref, ko_ref, kp_ref, vo_ref, vp_ref, ro_ref, rp_ref, o_ref, *, group, scale):
    n = pl.program_id(0)
    hk = pl.program_id(1)
    blk = q_ref.shape[0]
    half = LANES // 2

    def rope(x, r):
        c, s1, s2 = r[:, 0:LANES], r[:, LANES:2 * LANES], r[:, 2 * LANES:3 * LANES]
        return x * c + pltpu.roll(x, LANES - 8, 1) * s1 + pltpu.roll(x, 8, 1) * s2

    r_own = ro_ref[...]
    k = jnp.concatenate([kp_ref[...], ko_ref[...]], axis=0).astype(F32)
    k = rope(k, jnp.concatenate([rp_ref[...], r_own], axis=0))
    v = jnp.concatenate([vp_ref[...], vo_ref[...]], axis=0).astype(F32)
    odd = (hk % 2) == 1
    k = jnp.where(odd, pltpu.roll(k, half, 1), k)
    v = jnp.where(odd, pltpu.roll(v, half, 1), v)
    lane = lax.broadcasted_iota(jnp.int32, k.shape, 1)
    k_lo = jnp.where(lane < half, k, 0.0)
    v_lo = jnp.where(lane < half, v, 0.0)
    k_pad = (k_lo.astype(BF16), pltpu.roll(k_lo, half, 1).astype(BF16))
    v_pad = (v_lo.astype(BF16), pltpu.roll(v_lo, half, 1).astype(BF16))

    qi = lax.broadcasted_iota(jnp.int32, (blk, 2 * blk), 0)
    kj = lax.broadcasted_iota(jnp.int32, (blk, 2 * blk), 1)
    rel = qi + blk - kj
    valid = (rel >= 0) & (rel < WINDOW) & ((kj >= blk) | (n > 0))

    for t in range(group // 2):
        q = rope(q_ref[:, t * LANES:(t + 1) * LANES].astype(F32), r_own).astype(BF16)
        o_tile = None
        for hh in range(2):
            s = lax.dot_general(q, k_pad[hh], (((1,), (1,)), ((), ())), preferred_element_type=F32) * scale
            s = jnp.where(valid, s, MASK_VALUE)
            sink = sinks_ref[hk * group + 2 * t + hh]
            m = jnp.maximum(jnp.max(s, axis=1, keepdims=True), sink)
            p = jnp.exp(s - m)
            denom = jnp.sum(p, axis=1, keepdims=True) + jnp.exp(sink - m)
            p = (p / denom).astype(BF16)
            o = jnp.dot(p, v_pad[hh], preferred_element_type=F32)
            o_tile = o if o_tile is None else o_tile + o
        o_ref[:, t * LANES:(t + 1) * LANES] = o_tile.astype(o_ref.dtype)


def _rope_tables(l, head_dim):
    rope_dim = head_dim // 4
    hf = rope_dim // 2
    inv_freq = jnp.exp(-math.log(ROPE_THETA) * jnp.arange(hf, dtype=F32) * (2.0 / rope_dim))
    ang = jnp.arange(l).astype(F32)[:, None] * inv_freq[None, :]
    cos, sin = jnp.cos(ang), jnp.sin(ang)
    rest = head_dim - rope_dim
    c = jnp.concatenate([cos, cos, jnp.ones((l, rest), F32)], axis=1)
    s1 = jnp.concatenate([-sin, jnp.zeros((l, head_dim - hf), F32)], axis=1)
    s2 = jnp.concatenate([jnp.zeros((l, hf), F32), sin, jnp.zeros((l, rest), F32)], axis=1)
    reps = LANES // head_dim
    return jnp.concatenate([jnp.tile(c, (1, reps)), jnp.tile(s1, (1, reps)), jnp.tile(s2, (1, reps))], axis=1)


def _swa_attention(qkv, sinks, q_heads, kv_heads):
    l = qkv.shape[0]
    dh = SWA_HEAD_DIM
    group = q_heads // kv_heads
    qw = q_heads * dh
    kvw = kv_heads * dh
    assert dh * 2 == LANES and group % 2 == 0 and kv_heads % 2 == 0
    blk = ATTN_BLOCK
    nb = l // blk
    gw = group * dh
    k_col0 = qw // LANES
    v_col0 = (qw + kvw) // LANES
    rope = _rope_tables(l, dh)
    prev = lambda n: jnp.maximum(n - 1, 0)
    kern = functools.partial(_swa_kernel, group=group, scale=dh ** -0.5)
    return pl.pallas_call(
        kern, grid=(nb, kv_heads),
        in_specs=[pl.BlockSpec(memory_space=pltpu.SMEM),
                  pl.BlockSpec((blk, gw), lambda n, h: (n, h)),
                  pl.BlockSpec((blk, LANES), lambda n, h: (n, k_col0 + h // 2)),
                  pl.BlockSpec((blk, LANES), lambda n, h: (prev(n), k_col0 + h // 2)),
                  pl.BlockSpec((blk, LANES), lambda n, h: (n, v_col0 + h // 2)),
                  pl.BlockSpec((blk, LANES), lambda n, h: (prev(n), v_col0 + h // 2)),
                  pl.BlockSpec((blk, 3 * LANES), lambda n, h: (n, 0)),
                  pl.BlockSpec((blk, 3 * LANES), lambda n, h: (prev(n), 0))],
        out_specs=pl.BlockSpec((blk, gw), lambda n, h: (n, h)),
        out_shape=jax.ShapeDtypeStruct((l, qw), BF16),
        compiler_params=_params(("parallel", "arbitrary")), name="swa_attn",
    )(sinks.astype(F32), qkv, qkv, qkv, qkv, qkv, rope, rope)


def _logsig_cumsum_kernel(x_ref, b_ref, o_ref, carry_ref):
    i = pl.program_id(0)

    @pl.when(i == 0)
    def _():
        carry_ref[...] = jnp.zeros(carry_ref.shape, F32)

    x = x_ref[...] + b_ref[...]
    lf = jnp.minimum(x, 0.0) - jnp.log1p(jnp.exp(-jnp.abs(x)))
    rows = lf.shape[0]
    row = lax.broadcasted_iota(jnp.int32, lf.shape, 0)
    d = 1
    while d < rows:
        lf = lf + jnp.where(row >= d, pltpu.roll(lf, d, 0), 0.0)
        d *= 2
    lf = lf + carry_ref[0:1, :]
    o_ref[...] = lf
    carry_ref[...] = jnp.broadcast_to(lf[rows - 1:rows, :], carry_ref.shape)


def _logsig_cumsum(x, b):
    l, w = x.shape
    tc = _tile(l, 512, SUBLANES)
    return pl.pallas_call(
        _logsig_cumsum_kernel, grid=(l // tc,),
        in_specs=[pl.BlockSpec((tc, w), lambda i: (i, 0)), pl.BlockSpec((1, w), lambda i: (0, 0))],
        out_specs=pl.BlockSpec((tc, w), lambda i: (i, 0)),
        out_shape=jax.ShapeDtypeStruct((l, w), F32),
        scratch_shapes=[pltpu.VMEM((SUBLANES, w), F32)],
        compiler_params=_params(("arbitrary",)), name="logsig_cumsum",
    )(x, b)


def _fox_kernel(q_ref, k_ref, v_ref, cq_ref, ck_ref, o_ref, m_ref, l_ref, acc_ref, *, tq, tk, scale):
    h = pl.program_id(0)
    qi = pl.program_id(1)
    q = q_ref[...]
    lane = lax.broadcasted_iota(jnp.int32, cq_ref.shape, 1)
    cq = jnp.sum(jnp.where(lane == h, cq_ref[...], 0.0), axis=1, keepdims=True)
    m_ref[...] = jnp.full(m_ref.shape, MASK_VALUE, F32)
    l_ref[...] = jnp.zeros(l_ref.shape, F32)
    acc_ref[...] = jnp.zeros(acc_ref.shape, F32)

    def step(j, masked):
        r0 = pl.multiple_of(j * tk, tk)
        ks = k_ref[pl.ds(r0, tk), :]
        vs = v_ref[pl.ds(r0, tk), :]
        ck = ck_ref[0, pl.ds(j, 1), :]
        s = lax.dot_general(q, ks, (((1,), (1,)), ((), ())), preferred_element_type=F32) * scale
        s = s + cq - ck
        if masked:
            qpos = qi * tq + lax.broadcasted_iota(jnp.int32, s.shape, 0)
            kpos = j * tk + lax.broadcasted_iota(jnp.int32, s.shape, 1)
            s = jnp.where(kpos <= qpos, s, MASK_VALUE)
        m_old = m_ref[...]
        m_new = jnp.maximum(m_old, jnp.max(s, axis=1, keepdims=True))
        alpha = jnp.exp(m_old - m_new)
        p = jnp.exp(s - m_new)
        l_ref[...] = alpha * l_ref[...] + jnp.sum(p, axis=1, keepdims=True)
        acc_ref[...] = alpha * acc_ref[...] + jnp.dot(p.astype(BF16), vs, preferred_element_type=F32)
        m_ref[...] = m_new

    per_q = tq // tk
    n_full = qi * per_q

    def full_body(j, c):
        step(j, False)
        return c

    lax.fori_loop(0, n_full, full_body, 0)
    for d in range(per_q):
        step(n_full + d, True)
    o_ref[...] = (acc_ref[...] / l_ref[...]).astype(o_ref.dtype)


def _fox_attention(proj, cum, heads):
    l = proj.shape[0]
    dh = FOX_HEAD_DIM
    assert dh == LANES
    tq = _tile(l, 512, LANES)
    tk = tq
    cum_t = cum[:, :heads].T.reshape(heads, l // tk, tk)
    kern = functools.partial(_fox_kernel, tq=tq, tk=tk, scale=dh ** -0.5)
    return pl.pallas_call(
        kern, grid=(heads, l // tq),
        in_specs=[pl.BlockSpec((tq, dh), lambda h, i: (i, h)),
                  pl.BlockSpec((l, dh), lambda h, i: (0, heads + h)),
                  pl.BlockSpec((l, dh), lambda h, i: (0, 2 * heads + h)),
                  pl.BlockSpec((tq, cum.shape[1]), lambda h, i: (i, 0)),
                  pl.BlockSpec((1, l // tk, tk), lambda h, i: (h, 0, 0))],
        out_specs=pl.BlockSpec((tq, dh), lambda h, i: (i, h)),
        out_shape=jax.ShapeDtypeStruct((l, heads * dh), BF16),
        scratch_shapes=[pltpu.VMEM((tq, 1), F32), pltpu.VMEM((tq, 1), F32), pltpu.VMEM((tq, dh), F32)],
        compiler_params=_params(("parallel", "arbitrary")), name="fox_attn",
    )(proj, proj, proj, cum, cum_t)


S5_GROUPS_PER_BLOCK = 16


def _s5_kernel(u_ref, b_ref, tab_ref, c_ref, d_ref, o_ref, xs_ref, carry_ref, *, ns):
    c = pl.program_id(1)

    @pl.when(c == 0)
    def _():
        carry_ref[...] = jnp.zeros(carry_ref.shape, F32)

    u = u_ref[...]
    t = u.shape[0]
    xs_ref[...] = jnp.dot(u, b_ref[...], preferred_element_type=F32)
    s = SUBLANES

    def tab(k):
        return tab_ref[k * s:(k + 1) * s, :]

    def body(k, carry):
        cr, ci = carry
        r0 = pl.multiple_of(k * s, s)
        vr = xs_ref[pl.ds(r0, s), 0:ns]
        vi = xs_ref[pl.ds(r0, s), ns:2 * ns]
        for idx, d in enumerate((1, 2, 4)):
            ar, ai = tab(2 * idx), tab(2 * idx + 1)
            sr, si = pltpu.roll(vr, d, 0), pltpu.roll(vi, d, 0)
            vr, vi = vr + ar * sr - ai * si, vi + ar * si + ai * sr
        pr, pi = tab(6), tab(7)
        vr, vi = vr + pr * cr - pi * ci, vi + pr * ci + pi * cr
        xs_ref[pl.ds(r0, s), 0:ns] = vr
        xs_ref[pl.ds(r0, s), ns:2 * ns] = vi
        return vr[s - 1:s, :], vi[s - 1:s, :]

    cr, ci = lax.fori_loop(0, t // s, body, (carry_ref[0:1, 0:ns], carry_ref[0:1, ns:2 * ns]))
    carry_ref[0:1, 0:ns] = cr
    carry_ref[0:1, ns:2 * ns] = ci
    y = jnp.dot(xs_ref[...].astype(BF16), c_ref[...], preferred_element_type=F32)
    y = y + d_ref[...] * u.astype(F32)
    o_ref[...] = jax.nn.gelu(y, approximate=True).astype(o_ref.dtype)


def _s5_tables(lam_re, lam_im, log_dt, b_re, b_im, c_re, c_im):
    g, p = lam_re.shape
    hch = b_re.shape[2]
    gb = S5_GROUPS_PER_BLOCK
    nb = g // gb
    lre = jnp.minimum(lam_re.astype(F32), S5_MAX_RE)
    lim = lam_im.astype(F32)
    lam = lax.complex(lre, lim)
    dt = jnp.exp(log_dt.astype(F32))[:, None]
    lam_bar = jnp.exp(lam * dt)
    b_bar = ((lam_bar - 1.0) / lam)[..., None] * lax.complex(b_re.astype(F32), b_im.astype(F32))
    eye = jnp.eye(gb, dtype=F32)

    def in_map(x):
        return jnp.einsum('gaph,ab->gahbp', x.reshape(nb, gb, p, hch), eye).reshape(nb, gb * hch, gb * p)

    def out_map(x):
        return jnp.einsum('gahp,ab->gapbh', x.reshape(nb, gb, hch, p), eye).reshape(nb, gb * p, gb * hch)

    b_cat = jnp.concatenate([in_map(jnp.real(b_bar)), in_map(jnp.imag(b_bar))], axis=2).astype(BF16)
    c_cat = jnp.concatenate([out_map(c_re.astype(F32)), out_map(-c_im.astype(F32))], axis=1).astype(BF16)

    def power(k):
        mag = jnp.exp(k * lre * dt)
        ang = k * lim * dt
        return ((mag * jnp.cos(ang)).reshape(nb, 1, gb * p), (mag * jnp.sin(ang)).reshape(nb, 1, gb * p))

    row = jnp.arange(SUBLANES)[None, :, None]
    parts = []
    for d in (1, 2, 4):
        re, im = power(float(d))
        parts += [jnp.where(row >= d, re, 0.0), jnp.where(row >= d, im, 0.0)]
    pre = jnp.concatenate([power(float(r + 1))[0] for r in range(SUBLANES)], axis=1)
    pim = jnp.concatenate([power(float(r + 1))[1] for r in range(SUBLANES)], axis=1)
    parts += [pre, pim]
    tab = jnp.concatenate(parts, axis=1)
    return b_cat, c_cat, tab


def _s5_core(hn, lam_re, lam_im, log_dt, b_re, b_im, c_re, c_im, d):
    l, dm = hn.shape
    gb = S5_GROUPS_PER_BLOCK
    b_cat, c_cat, tab = _s5_tables(lam_re, lam_im, log_dt, b_re, b_im, c_re, c_im)
    nb = b_cat.shape[0]
    wu = gb * S5_GROUP
    ns = gb * S5_STATE
    t = _tile(l, 512, 16)
    kern = functools.partial(_s5_kernel, ns=ns)
    return pl.pallas_call(
        kern, grid=(nb, l // t),
        in_specs=[pl.BlockSpec((t, wu), lambda g, c: (c, g)),
                  pl.BlockSpec((None, wu, 2 * ns), lambda g, c: (g, 0, 0)),
                  pl.BlockSpec((None, 8 * SUBLANES, ns), lambda g, c: (g, 0, 0)),
                  pl.BlockSpec((None, 2 * ns, wu), lambda g, c: (g, 0, 0)),
                  pl.BlockSpec((1, wu), lambda g, c: (0, g))],
        out_specs=pl.BlockSpec((t, wu), lambda g, c: (c, g)),
        out_shape=jax.ShapeDtypeStruct((l, dm), BF16),
        scratch_shapes=[pltpu.VMEM((t, 2 * ns), F32), pltpu.VMEM((SUBLANES, 2 * ns), F32)],
        compiler_params=_params(("parallel", "arbitrary")), name="s5_core",
    )(hn, b_cat, tab, c_cat, d.reshape(1, dm).astype(F32))


def _swa_mixer(hn, w_qkv, b_qkv, sinks, w_o, b_o):
    q_heads = sinks.shape[0]
    kv_heads = q_heads // SWA_KV_HEADS_PER_Q
    qkv = _mm(hn, w_qkv.astype(BF16), b_qkv, BF16, tn_pref=512, name="swa_qkv")
    o = _swa_attention(qkv, sinks, q_heads, kv_heads)
    return _mm(o, w_o.astype(BF16), b_o, F32, name="swa_out")


def _fox_mixer(hn, w_qkvf, b_f, w_o):
    heads = b_f.shape[0]
    hw = heads * FOX_HEAD_DIM
    proj = _mm(hn, w_qkvf[:, :3 * hw].astype(BF16), None, BF16, name="fox_qkv")
    pad = LANES - heads
    w_f = jnp.pad(w_qkvf[:, 3 * hw:], ((0, 0), (0, pad))).astype(BF16)
    logits = _mm(hn, w_f, None, F32, name="fox_gate")
    cum = _logsig_cumsum(logits, jnp.pad(b_f.astype(F32), (0, pad)).reshape(1, LANES))
    o = _fox_attention(proj, cum, heads)
    return _mm(o, w_o.astype(BF16), None, F32, name="fox_out")


def _s5_mixer(hn, lam_re, lam_im, log_dt, b_re, b_im, c_re, c_im, d, w_glu, b_glu):
    g = _s5_core(hn, lam_re, lam_im, log_dt, b_re, b_im, c_re, c_im, d)
    return _glu_mm(g, w_glu.astype(BF16), b_glu, F32)


def _conv_ffn(hn, w_in, conv_w, conv_b, w_out):
    g = _ffn_in(hn, w_in.astype(BF16), conv_w.astype(F32), conv_b.astype(F32))
    return _mm(g, w_out.astype(BF16), None, F32, tm_pref=512, tn_pref=512, name="ffn_out")


def _trunk(x, layers):
    b, l, d = x.shape
    assert b == 1
    h = x.reshape(l, d)
    mixers = (_swa_mixer, _fox_mixer, _s5_mixer)
    hn = _prenorm(h, layers[0][0])
    for i, (mix_pre, mix_params, mix_post, ffn_pre, ffn_params, ffn_post) in enumerate(layers):
        y = mixers[i % len(mixers)](hn, *mix_params)
        h, hn = _post_pre(y, h, mix_post, ffn_pre)
        y = _conv_ffn(hn, *ffn_params)
        nxt = layers[i + 1][0] if i + 1 < len(layers) else None
        h, hn = _post_pre(y, h, ffn_post, nxt)
    return h.reshape(b, l, d)


def kernel(x,
           l0_mix_pre_g, l0_swa_w_qkv, l0_swa_b_qkv, l0_swa_sinks, l0_swa_w_o, l0_swa_b_o, l0_mix_post_g,
           l0_ffn_pre_g, l0_ffn_w_in, l0_ffn_conv_w, l0_ffn_conv_b, l0_ffn_w_out, l0_ffn_post_g,
           l1_mix_pre_g, l1_fox_w_qkvf, l1_fox_b_f, l1_fox_w_o, l1_mix_post_g,
           l1_ffn_pre_g, l1_ffn_w_in, l1_ffn_conv_w, l1_ffn_conv_b, l1_ffn_w_out, l1_ffn_post_g,
           l2_mix_pre_g, l2_s5_lambda_re, l2_s5_lambda_im, l2_s5_log_dt, l2_s5_b_re, l2_s5_b_im,
           l2_s5_c_re, l2_s5_c_im, l2_s5_d, l2_s5_w_glu, l2_s5_b_glu, l2_mix_post_g,
           l2_ffn_pre_g, l2_ffn_w_in, l2_ffn_conv_w, l2_ffn_conv_b, l2_ffn_w_out, l2_ffn_post_g,
           l3_mix_pre_g, l3_swa_w_qkv, l3_swa_b_qkv, l3_swa_sinks, l3_swa_w_o, l3_swa_b_o, l3_mix_post_g,
           l3_ffn_pre_g, l3_ffn_w_in, l3_ffn_conv_w, l3_ffn_conv_b, l3_ffn_w_out, l3_ffn_post_g):
    layers = (
        (l0_mix_pre_g, (l0_swa_w_qkv, l0_swa_b_qkv, l0_swa_sinks, l0_swa_w_o, l0_swa_b_o), l0_mix_post_g,
         l0_ffn_pre_g, (l0_ffn_w_in, l0_ffn_conv_w, l0_ffn_conv_b, l0_ffn_w_out), l0_ffn_post_g),
        (l1_mix_pre_g, (l1_fox_w_qkvf, l1_fox_b_f, l1_fox_w_o), l1_mix_post_g,
         l1_ffn_pre_g, (l1_ffn_w_in, l1_ffn_conv_w, l1_ffn_conv_b, l1_ffn_w_out), l1_ffn_post_g),
        (l2_mix_pre_g, (l2_s5_lambda_re, l2_s5_lambda_im, l2_s5_log_dt, l2_s5_b_re, l2_s5_b_im,
                        l2_s5_c_re, l2_s5_c_im, l2_s5_d, l2_s5_w_glu, l2_s5_b_glu), l2_mix_post_g,
         l2_ffn_pre_g, (l2_ffn_w_in, l2_ffn_conv_w, l2_ffn_conv_b, l2_ffn_w_out), l2_ffn_post_g),
        (l3_mix_pre_g, (l3_swa_w_qkv, l3_swa_b_qkv, l3_swa_sinks, l3_swa_w_o, l3_swa_b_o), l3_mix_post_g,
         l3_ffn_pre_g, (l3_ffn_w_in, l3_ffn_conv_w, l3_ffn_conv_b, l3_ffn_w_out), l3_ffn_post_g),
    )
    return _trunk(x, layers)
```

```python
import functools
import math

import jax
import jax.numpy as jnp
from jax import lax
from jax.experimental import pallas as pl
from jax.experimental.pallas import tpu as pltpu

F32 = jnp.float32
BF16 = jnp.bfloat16

NORM_EPS = 1e-6
MASK_VALUE = -1e30
LOG2E = math.log2(math.e)
ATTN_BLOCK = 128
WINDOW = 128
ROPE_THETA = 500000.0
SWA_KV_HEADS_PER_Q = 8
SWA_HEAD_DIM = 64
FOX_HEAD_DIM = 128
S5_GROUP = 16
S5_STATE = 64
S5_MAX_RE = -1e-4
CONV_WIDTH = 3

LANES = 128
SUBLANES = 8
VMEM_LIMIT_BYTES = 56 * 1024 * 1024


def _params(sem, vmem=VMEM_LIMIT_BYTES):
    return pltpu.CompilerParams(dimension_semantics=sem, vmem_limit_bytes=vmem)


def _tile(dim, pref, mult):
    if dim <= pref:
        return dim
    t = (pref // mult) * mult
    while t > mult and dim % t:
        t -= mult
    assert dim % t == 0, (dim, pref, mult)
    return t


def _rms(x, g):
    ms = jnp.mean(x * x, axis=-1, keepdims=True)
    return x * lax.rsqrt(ms + NORM_EPS) * g


def _prenorm_kernel(x_ref, g_ref, o_ref):
    o_ref[...] = _rms(x_ref[...], g_ref[...]).astype(o_ref.dtype)


def _prenorm(x, g):
    m, d = x.shape
    tr = _tile(m, 256, SUBLANES)
    return pl.pallas_call(
        _prenorm_kernel,
        grid=(m // tr,),
        in_specs=[pl.BlockSpec((tr, d), lambda i: (i, 0)), pl.BlockSpec((1, d), lambda i: (0, 0))],
        out_specs=pl.BlockSpec((tr, d), lambda i: (i, 0)),
        out_shape=jax.ShapeDtypeStruct((m, d), BF16),
        compiler_params=_params(("parallel",)),
        name="prenorm",
    )(x, g.reshape(1, d))


def _post_pre_kernel(y_ref, h_ref, gp_ref, gn_ref, h_out_ref, hn_ref):
    h_new = h_ref[...] + _rms(y_ref[...].astype(F32), gp_ref[...])
    h_out_ref[...] = h_new
    hn_ref[...] = _rms(h_new, gn_ref[...]).astype(hn_ref.dtype)


def _post_kernel(y_ref, h_ref, gp_ref, h_out_ref):
    h_out_ref[...] = h_ref[...] + _rms(y_ref[...].astype(F32), gp_ref[...])


def _post_pre(y, h, g_post, g_next):
    m, d = h.shape
    tr = _tile(m, 256, SUBLANES)
    row = pl.BlockSpec((tr, d), lambda i: (i, 0))
    vec = pl.BlockSpec((1, d), lambda i: (0, 0))
    if g_next is None:
        return pl.pallas_call(
            _post_kernel, grid=(m // tr,), in_specs=[row, row, vec], out_specs=row,
            out_shape=jax.ShapeDtypeStruct((m, d), F32),
            compiler_params=_params(("parallel",)), name="post",
        )(y, h, g_post.reshape(1, d)), None
    return pl.pallas_call(
        _post_pre_kernel, grid=(m // tr,), in_specs=[row, row, vec, vec], out_specs=[row, row],
        out_shape=[jax.ShapeDtypeStruct((m, d), F32), jax.ShapeDtypeStruct((m, d), BF16)],
        compiler_params=_params(("parallel",)), name="post_pre",
    )(y, h, g_post.reshape(1, d), g_next.reshape(1, d))


def _mm_kernel(x_ref, w_ref, o_ref):
    o_ref[...] = jnp.dot(x_ref[...], w_ref[...], preferred_element_type=F32).astype(o_ref.dtype)


def _mm_bias_kernel(x_ref, w_ref, b_ref, o_ref):
    acc = jnp.dot(x_ref[...], w_ref[...], preferred_element_type=F32) + b_ref[...]
    o_ref[...] = acc.astype(o_ref.dtype)


def _mm(x, w, b, out_dtype, tm_pref=1024, tn_pref=1024, name="mm"):
    m, k = x.shape
    n = w.shape[1]
    tm = _tile(m, tm_pref, 16)
    tn = _tile(n, tn_pref, LANES)
    in_specs = [pl.BlockSpec((tm, k), lambda i, j: (i, 0)), pl.BlockSpec((k, tn), lambda i, j: (0, j))]
    args = [x, w]
    body = _mm_kernel
    if b is not None:
        in_specs.append(pl.BlockSpec((1, tn), lambda i, j: (0, j)))
        args.append(b.reshape(1, n).astype(F32))
        body = _mm_bias_kernel
    return pl.pallas_call(
        body, grid=(m // tm, n // tn), in_specs=in_specs,
        out_specs=pl.BlockSpec((tm, tn), lambda i, j: (i, j)),
        out_shape=jax.ShapeDtypeStruct((m, n), out_dtype),
        compiler_params=_params(("parallel", "arbitrary")), name=name,
    )(*args)


def _glu_kernel(x_ref, w1_ref, w2_ref, b1_ref, b2_ref, o_ref):
    x = x_ref[...]
    z1 = jnp.dot(x, w1_ref[...], preferred_element_type=F32) + b1_ref[...]
    z2 = jnp.dot(x, w2_ref[...], preferred_element_type=F32) + b2_ref[...]
    o_ref[...] = (z1 * jax.nn.sigmoid(z2)).astype(o_ref.dtype)


def _glu_mm(x, w, b, out_dtype):
    m, k = x.shape
    n = w.shape[1] // 2
    tm = _tile(m, 1024, 16)
    tn = _tile(n, 512, LANES)
    nj = n // tn
    b2 = b.reshape(1, 2 * n).astype(F32)
    return pl.pallas_call(
        _glu_kernel, grid=(m // tm, nj),
        in_specs=[pl.BlockSpec((tm, k), lambda i, j: (i, 0)),
                  pl.BlockSpec((k, tn), lambda i, j: (0, j)),
                  pl.BlockSpec((k, tn), lambda i, j: (0, j + nj)),
                  pl.BlockSpec((1, tn), lambda i, j: (0, j)),
                  pl.BlockSpec((1, tn), lambda i, j: (0, j + nj))],
        out_specs=pl.BlockSpec((tm, tn), lambda i, j: (i, j)),
        out_shape=jax.ShapeDtypeStruct((m, n), out_dtype),
        compiler_params=_params(("parallel", "arbitrary")), name="glu_mm",
    )(x, w, w, b2, b2)


def _ffn_in_kernel(x_ref, wg_ref, wu_ref, cwg_ref, cwu_ref, cbg_ref, cbu_ref, o_ref, carry_g, carry_u,
                   zg_ref, zu_ref, *, sub):
    i = pl.program_id(0)
    j = pl.program_id(1)
    tm = x_ref.shape[0]
    tn = o_ref.shape[1]
    s = SUBLANES
    nt = sub // s

    @pl.when(i == 0)
    def _():
        carry_g[j] = jnp.zeros(carry_g.shape[1:], F32)
        carry_u[j] = jnp.zeros(carry_u.shape[1:], F32)

    sidx = lax.broadcasted_iota(jnp.int32, (nt, s, tn), 1)

    def conv(z, prev, cw_ref, cb_ref):
        z3 = z.reshape(nt, s, tn)

        def shifted(d):
            r = pltpu.roll(z3, d, 1)
            rp = jnp.concatenate([pltpu.roll(prev, d, 0)[None], r[:-1]], axis=0)
            return jnp.where(sidx < d, rp, r)

        cw = cw_ref[...]
        out = cb_ref[...] + cw[2:3, :] * z3 + cw[0:1, :] * shifted(2) + cw[1:2, :] * shifted(1)
        return out.reshape(sub, tn), z3[nt - 1]

    wg = wg_ref[...]
    wu = wu_ref[...]
    prev_g = carry_g[j]
    prev_u = carry_u[j]
    zero = jnp.minimum(j, 0)

    def dots(r):
        x = x_ref[r * sub:(r + 1) * sub, :]
        zg_ref[zero + r % 2] = jnp.dot(x, wg, preferred_element_type=F32)
        zu_ref[zero + r % 2] = jnp.dot(x, wu, preferred_element_type=F32)

    nsub = tm // sub
    dots(0)
    for r in range(nsub):
        if r + 1 < nsub:
            dots(r + 1)
        gate, prev_g = conv(zg_ref[zero + r % 2], prev_g, cwg_ref, cbg_ref)
        up, prev_u = conv(zu_ref[zero + r % 2], prev_u, cwu_ref, cbu_ref)
        o_ref[r * sub:(r + 1) * sub, :] = (jax.nn.gelu(gate, approximate=True) * up).astype(o_ref.dtype)
    carry_g[j] = prev_g
    carry_u[j] = prev_u


def _ffn_in(x, w_in, conv_w, conv_b):
    m, k = x.shape
    dff = w_in.shape[1] // 2
    tm = _tile(m, 1024, 16)
    tn = _tile(dff, 256, LANES)
    nj = dff // tn
    cb = conv_b.reshape(1, 2 * dff)
    sub = _tile(tm, 128, 16)
    kern = functools.partial(_ffn_in_kernel, sub=sub)
    return pl.pallas_call(
        kern, grid=(m // tm, nj),
        in_specs=[pl.BlockSpec((tm, k), lambda i, j: (i, 0)),
                  pl.BlockSpec((k, tn), lambda i, j: (0, j)),
                  pl.BlockSpec((k, tn), lambda i, j: (0, j + nj)),
                  pl.BlockSpec((CONV_WIDTH, tn), lambda i, j: (0, j)),
                  pl.BlockSpec((CONV_WIDTH, tn), lambda i, j: (0, j + nj)),
                  pl.BlockSpec((1, tn), lambda i, j: (0, j)),
                  pl.BlockSpec((1, tn), lambda i, j: (0, j + nj))],
        out_specs=pl.BlockSpec((tm, tn), lambda i, j: (i, j)),
        out_shape=jax.ShapeDtypeStruct((m, dff), BF16),
        scratch_shapes=[pltpu.VMEM((nj, SUBLANES, tn), F32), pltpu.VMEM((nj, SUBLANES, tn), F32),
                        pltpu.VMEM((2, sub, tn), F32), pltpu.VMEM((2, sub, tn), F32)],
        compiler_params=_params(("arbitrary", "arbitrary")), name="ffn_in",
    )(x, w_in, w_in, conv_w, conv_w, cb, cb)


def _swa_kernel(sinks_ref, q_ref, ko_ref, kp_ref, vo_ref, vp_ref, ro_ref, rp_ref, o_ref, *, group, scale):
    n = pl.program_id(0)
    hk = pl.program_id(1)
    blk = q_ref.shape[0]
    half = LANES // 2

    def rope(x, r):
        c, s1, s2 = r[:, 0:LANES], r[:, LANES:2 * LANES], r[:, 2 * LANES:3 * LANES]
        return x * c + pltpu.roll(x, LANES - 8, 1) * s1 + pltpu.roll(x, 8, 1) * s2

    r_own = ro_ref[...]
    k = jnp.concatenate([kp_ref[...], ko_ref[...]], axis=0).astype(F32)
    k = rope(k, jnp.concatenate([rp_ref[...], r_own], axis=0))
    v = jnp.concatenate([vp_ref[...], vo_ref[...]], axis=0).astype(F32)
    odd = (hk % 2) == 1
    k = jnp.where(odd, pltpu.roll(k, half, 1), k)
    v = jnp.where(odd, pltpu.roll(v, half, 1), v)
    lane = lax.broadcasted_iota(jnp.int32, k.shape, 1)
    k_lo = jnp.where(lane < half, k, 0.0)
    v_lo = jnp.where(lane < half, v, 0.0)
    k_pad = (k_lo.astype(BF16), pltpu.roll(k_lo, half, 1).astype(BF16))
    v_pad = (v_lo.astype(BF16), pltpu.roll(v_lo, half, 1).astype(BF16))

    nk = 2 * blk
    qi = lax.broadcasted_iota(jnp.int32, (blk, group * nk), 0)
    assert nk & (nk - 1) == 0
    kj = lax.broadcasted_iota(jnp.int32, (blk, group * nk), 1) & (nk - 1)
    rel = qi + blk - kj
    valid = (rel >= 0) & (rel < WINDOW) & ((kj >= blk) | (n > 0))

    tiles = []
    for t in range(group // 2):
        q = rope(q_ref[:, t * LANES:(t + 1) * LANES].astype(F32), r_own)
        q = (q * (scale * LOG2E)).astype(BF16)
        for hh in range(2):
            tiles.append(lax.dot_general(q, k_pad[hh], (((1,), (1,)), ((), ())), preferred_element_type=F32))
    s = jnp.where(valid, jnp.concatenate(tiles, axis=1), MASK_VALUE)

    def per_head(fn):
        return jnp.concatenate([jnp.broadcast_to(fn(g), (blk, nk)) for g in range(group)], axis=1)

    sinks = [sinks_ref[hk * group + g] * LOG2E for g in range(group)]
    m = [jnp.maximum(jnp.max(s[:, g * nk:(g + 1) * nk], axis=1, keepdims=True), sinks[g]) for g in range(group)]
    p = jnp.exp2(s - per_head(lambda g: m[g]))
    inv = per_head(lambda g: 1.0 / (jnp.sum(p[:, g * nk:(g + 1) * nk], axis=1, keepdims=True)
                                    + jnp.exp2(sinks[g] - m[g])))
    p = (p * inv).astype(BF16)
    for t in range(group // 2):
        g0 = 2 * t
        o_tile = (jnp.dot(p[:, g0 * nk:(g0 + 1) * nk], v_pad[0], preferred_element_type=F32)
                  + jnp.dot(p[:, (g0 + 1) * nk:(g0 + 2) * nk], v_pad[1], preferred_element_type=F32))
        o_ref[:, t * LANES:(t + 1) * LANES] = o_tile.astype(o_ref.dtype)


def _rope_tables(l, head_dim):
    rope_dim = head_dim // 4
    hf = rope_dim // 2
    inv_freq = jnp.exp(-math.log(ROPE_THETA) * jnp.arange(hf, dtype=F32) * (2.0 / rope_dim))
    ang = jnp.arange(l).astype(F32)[:, None] * inv_freq[None, :]
    cos, sin = jnp.cos(ang), jnp.sin(ang)
    rest = head_dim - rope_dim
    c = jnp.concatenate([cos, cos, jnp.ones((l, rest), F32)], axis=1)
    s1 = jnp.concatenate([-sin, jnp.zeros((l, head_dim - hf), F32)], axis=1)
    s2 = jnp.concatenate([jnp.zeros((l, hf), F32), sin, jnp.zeros((l, rest), F32)], axis=1)
    reps = LANES // head_dim
    return jnp.concatenate([jnp.tile(c, (1, reps)), jnp.tile(s1, (1, reps)), jnp.tile(s2, (1, reps))], axis=1)


def _swa_attention(qkv, sinks, q_heads, kv_heads):
    l = qkv.shape[0]
    dh = SWA_HEAD_DIM
    group = q_heads // kv_heads
    qw = q_heads * dh
    kvw = kv_heads * dh
    assert dh * 2 == LANES and group % 2 == 0 and kv_heads % 2 == 0
    blk = ATTN_BLOCK
    nb = l // blk
    gw = group * dh
    k_col0 = qw // LANES
    v_col0 = (qw + kvw) // LANES
    rope = _rope_tables(l, dh)
    prev = lambda n: jnp.maximum(n - 1, 0)
    kern = functools.partial(_swa_kernel, group=group, scale=dh ** -0.5)
    return pl.pallas_call(
        kern, grid=(nb, kv_heads),
        in_specs=[pl.BlockSpec(memory_space=pltpu.SMEM),
                  pl.BlockSpec((blk, gw), lambda n, h: (n, h)),
                  pl.BlockSpec((blk, LANES), lambda n, h: (n, k_col0 + h // 2)),
                  pl.BlockSpec((blk, LANES), lambda n, h: (prev(n), k_col0 + h // 2)),
                  pl.BlockSpec((blk, LANES), lambda n, h: (n, v_col0 + h // 2)),
                  pl.BlockSpec((blk, LANES), lambda n, h: (prev(n), v_col0 + h // 2)),
                  pl.BlockSpec((blk, 3 * LANES), lambda n, h: (n, 0)),
                  pl.BlockSpec((blk, 3 * LANES), lambda n, h: (prev(n), 0))],
        out_specs=pl.BlockSpec((blk, gw), lambda n, h: (n, h)),
        out_shape=jax.ShapeDtypeStruct((l, qw), BF16),
        compiler_params=_params(("parallel", "arbitrary")), name="swa_attn",
    )(sinks.astype(F32), qkv, qkv, qkv, qkv, qkv, rope, rope)


def _logsig_cumsum_kernel(x_ref, b_ref, c_ref, hi_ref, mid_ref, lo_ref, carry_ref):
    i = pl.program_id(0)

    @pl.when(i == 0)
    def _():
        carry_ref[...] = jnp.zeros(carry_ref.shape, F32)

    x = x_ref[...] + b_ref[...]
    lf = jnp.minimum(x, 0.0) - jnp.log1p(jnp.exp(-jnp.abs(x)))
    rows = lf.shape[0]
    row = lax.broadcasted_iota(jnp.int32, lf.shape, 0)
    d = 1
    while d < rows:
        lf = lf + jnp.where(row >= d, pltpu.roll(lf, d, 0), 0.0)
        d *= 2
    lf = lf + carry_ref[0:1, :]
    carry_ref[...] = jnp.broadcast_to(lf[rows - 1:rows, :], carry_ref.shape)
    c = lf * LOG2E
    c_ref[...] = c
    neg = -c
    hi = neg.astype(BF16)
    rem = neg - hi.astype(F32)
    mid = rem.astype(BF16)
    hi_ref[...] = hi
    mid_ref[...] = mid
    lo_ref[...] = (rem - mid.astype(F32)).astype(BF16)


def _logsig_cumsum(x, b):
    l, w = x.shape
    tc = _tile(l, 512, 16)
    blk = pl.BlockSpec((tc, w), lambda i: (i, 0))
    return pl.pallas_call(
        _logsig_cumsum_kernel, grid=(l // tc,),
        in_specs=[blk, pl.BlockSpec((1, w), lambda i: (0, 0))],
        out_specs=[blk, blk, blk, blk],
        out_shape=[jax.ShapeDtypeStruct((l, w), F32)] + [jax.ShapeDtypeStruct((l, w), BF16)] * 3,
        scratch_shapes=[pltpu.VMEM((SUBLANES, w), F32)],
        compiler_params=_params(("arbitrary",)), name="logsig_cumsum",
    )(x, b)


def _fox_kernel(qt_ref, k_ref, a_ref, vt_ref, cq_ref, o_ref, m_ref, l_ref, acc_ref, *, tq, tk):
    h = pl.program_id(0)
    qi = pl.program_id(1)
    row = lax.broadcasted_iota(jnp.int32, (LANES, tq), 0)
    sel = jnp.where((row >= 3 * h) & (row < 3 * h + 3), 1.0, 0.0).astype(BF16)
    qa = jnp.concatenate([qt_ref[...], sel], axis=0)
    cq = cq_ref[...]
    m_ref[...] = jnp.full(m_ref.shape, MASK_VALUE, F32)
    l_ref[...] = jnp.zeros(l_ref.shape, F32)
    acc_ref[...] = jnp.zeros(acc_ref.shape, F32)

    def step(j, masked):
        r0 = pl.multiple_of(j * tk, tk)
        ka = jnp.concatenate([k_ref[pl.ds(r0, tk), :], a_ref[pl.ds(r0, tk), :]], axis=1)
        t = jnp.dot(ka, qa, preferred_element_type=F32)
        if masked:
            kpos = j * tk + lax.broadcasted_iota(jnp.int32, t.shape, 0)
            qpos = qi * tq + lax.broadcasted_iota(jnp.int32, t.shape, 1)
            t = jnp.where(kpos <= qpos, t, MASK_VALUE)
        m_old = m_ref[...]
        m_new = jnp.maximum(m_old, jnp.max(t, axis=0, keepdims=True) + cq)
        alpha = jnp.exp2(m_old - m_new)
        p = jnp.exp2(t - (m_new - cq))
        l_ref[...] = alpha * l_ref[...] + jnp.sum(p, axis=0, keepdims=True)
        acc_ref[...] = alpha * acc_ref[...] + jnp.dot(vt_ref[j], p.astype(BF16), preferred_element_type=F32)
        m_ref[...] = m_new

    per_q = tq // tk

    def full_body(it, c):
        for d in range(per_q):
            step(it * per_q + d, False)
        return c

    lax.fori_loop(0, qi, full_body, 0)
    for d in range(per_q):
        step(qi * per_q + d, True)
    o_ref[...] = (acc_ref[...] / l_ref[...]).T.astype(o_ref.dtype)


def _fox_attention(proj, c, pieces, heads):
    l = proj.shape[0]
    dh = FOX_HEAD_DIM
    hw = heads * dh
    assert dh == LANES and 3 * heads <= LANES
    tq = _tile(l, 1024, LANES)
    tk = _tile(tq, 512, LANES)
    qt = proj[:, :hw].T
    vt = proj[:, 2 * hw:].reshape(l // tk, tk, heads, dh).transpose(2, 0, 3, 1)
    a = jnp.stack([x[:, :heads] for x in pieces], axis=-1).reshape(l, 3 * heads)
    a = jnp.pad(a, ((0, 0), (0, LANES - 3 * heads)))
    cq = c[:, :heads].T.reshape(heads, 1, l)
    kern = functools.partial(_fox_kernel, tq=tq, tk=tk)
    return pl.pallas_call(
        kern, grid=(heads, l // tq),
        in_specs=[pl.BlockSpec((dh, tq), lambda h, i: (h, i)),
                  pl.BlockSpec((l, dh), lambda h, i: (0, heads + h)),
                  pl.BlockSpec((l, LANES), lambda h, i: (0, 0)),
                  pl.BlockSpec((None, l // tk, dh, tk), lambda h, i: (h, 0, 0, 0)),
                  pl.BlockSpec((None, 1, tq), lambda h, i: (h, 0, i))],
        out_specs=pl.BlockSpec((tq, dh), lambda h, i: (i, h)),
        out_shape=jax.ShapeDtypeStruct((l, hw), BF16),
        scratch_shapes=[pltpu.VMEM((1, tq), F32), pltpu.VMEM((1, tq), F32), pltpu.VMEM((dh, tq), F32)],
        compiler_params=_params(("parallel", "arbitrary")), name="fox_attn",
    )(qt, proj, a, vt, cq)


S5_GROUPS_PER_BLOCK = 16


def _s5_kernel(u_ref, b_ref, tab_ref, c_ref, d_ref, o_ref, xs_ref, carry_ref, *, ns):
    c = pl.program_id(1)

    @pl.when(c == 0)
    def _():
        carry_ref[...] = jnp.zeros(carry_ref.shape, F32)

    u = u_ref[...]
    t = u.shape[0]
    xs_ref[...] = jnp.dot(u, b_ref[...], preferred_element_type=F32)
    s = SUBLANES

    def tab(k):
        return tab_ref[k * s:(k + 1) * s, :]

    def body(k, carry):
        cr, ci = carry
        r0 = pl.multiple_of(k * s, s)
        vr = xs_ref[pl.ds(r0, s), 0:ns]
        vi = xs_ref[pl.ds(r0, s), ns:2 * ns]
        for idx, d in enumerate((1, 2, 4)):
            ar, ai = tab(2 * idx), tab(2 * idx + 1)
            sr, si = pltpu.roll(vr, d, 0), pltpu.roll(vi, d, 0)
            vr, vi = vr + ar * sr - ai * si, vi + ar * si + ai * sr
        pr, pi = tab(6), tab(7)
        vr, vi = vr + pr * cr - pi * ci, vi + pr * ci + pi * cr
        xs_ref[pl.ds(r0, s), 0:ns] = vr
        xs_ref[pl.ds(r0, s), ns:2 * ns] = vi
        return vr[s - 1:s, :], vi[s - 1:s, :]

    cr, ci = lax.fori_loop(0, t // s, body, (carry_ref[0:1, 0:ns], carry_ref[0:1, ns:2 * ns]))
    carry_ref[0:1, 0:ns] = cr
    carry_ref[0:1, ns:2 * ns] = ci
    y = jnp.dot(xs_ref[...].astype(BF16), c_ref[...], preferred_element_type=F32)
    y = y + d_ref[...] * u.astype(F32)
    o_ref[...] = jax.nn.gelu(y, approximate=True).astype(o_ref.dtype)


def _s5_tables(lam_re, lam_im, log_dt, b_re, b_im, c_re, c_im):
    g, p = lam_re.shape
    hch = b_re.shape[2]
    gb = S5_GROUPS_PER_BLOCK
    nb = g // gb
    lre = jnp.minimum(lam_re.astype(F32), S5_MAX_RE)
    lim = lam_im.astype(F32)
    lam = lax.complex(lre, lim)
    dt = jnp.exp(log_dt.astype(F32))[:, None]
    lam_bar = jnp.exp(lam * dt)
    b_bar = ((lam_bar - 1.0) / lam)[..., None] * lax.complex(b_re.astype(F32), b_im.astype(F32))
    eye = jnp.eye(gb, dtype=F32)

    def in_map(x):
        return jnp.einsum('gaph,ab->gahbp', x.reshape(nb, gb, p, hch), eye).reshape(nb, gb * hch, gb * p)

    def out_map(x):
        return jnp.einsum('gahp,ab->gapbh', x.reshape(nb, gb, hch, p), eye).reshape(nb, gb * p, gb * hch)

    b_cat = jnp.concatenate([in_map(jnp.real(b_bar)), in_map(jnp.imag(b_bar))], axis=2).astype(BF16)
    c_cat = jnp.concatenate([out_map(c_re.astype(F32)), out_map(-c_im.astype(F32))], axis=1).astype(BF16)

    def power(k):
        mag = jnp.exp(k * lre * dt)
        ang = k * lim * dt
        return ((mag * jnp.cos(ang)).reshape(nb, 1, gb * p), (mag * jnp.sin(ang)).reshape(nb, 1, gb * p))

    row = jnp.arange(SUBLANES)[None, :, None]
    parts = []
    for d in (1, 2, 4):
        re, im = power(float(d))
        parts += [jnp.where(row >= d, re, 0.0), jnp.where(row >= d, im, 0.0)]
    pre = jnp.concatenate([power(float(r + 1))[0] for r in range(SUBLANES)], axis=1)
    pim = jnp.concatenate([power(float(r + 1))[1] for r in range(SUBLANES)], axis=1)
    parts += [pre, pim]
    tab = jnp.concatenate(parts, axis=1)
    return b_cat, c_cat, tab


def _s5_core(hn, lam_re, lam_im, log_dt, b_re, b_im, c_re, c_im, d):
    l, dm = hn.shape
    gb = S5_GROUPS_PER_BLOCK
    b_cat, c_cat, tab = _s5_tables(lam_re, lam_im, log_dt, b_re, b_im, c_re, c_im)
    nb = b_cat.shape[0]
    wu = gb * S5_GROUP
    ns = gb * S5_STATE
    t = _tile(l, 512, 16)
    kern = functools.partial(_s5_kernel, ns=ns)
    return pl.pallas_call(
        kern, grid=(nb, l // t),
        in_specs=[pl.BlockSpec((t, wu), lambda g, c: (c, g)),
                  pl.BlockSpec((None, wu, 2 * ns), lambda g, c: (g, 0, 0)),
                  pl.BlockSpec((None, 8 * SUBLANES, ns), lambda g, c: (g, 0, 0)),
                  pl.BlockSpec((None, 2 * ns, wu), lambda g, c: (g, 0, 0)),
                  pl.BlockSpec((1, wu), lambda g, c: (0, g))],
        out_specs=pl.BlockSpec((t, wu), lambda g, c: (c, g)),
        out_shape=jax.ShapeDtypeStruct((l, dm), BF16),
        scratch_shapes=[pltpu.VMEM((t, 2 * ns), F32), pltpu.VMEM((SUBLANES, 2 * ns), F32)],
        compiler_params=_params(("parallel", "arbitrary")), name="s5_core",
    )(hn, b_cat, tab, c_cat, d.reshape(1, dm).astype(F32))


def _swa_mixer(hn, w_qkv, b_qkv, sinks, w_o, b_o):
    q_heads = sinks.shape[0]
    kv_heads = q_heads // SWA_KV_HEADS_PER_Q
    qkv = _mm(hn, w_qkv.astype(BF16), b_qkv, BF16, tn_pref=512, name="swa_qkv")
    o = _swa_attention(qkv, sinks, q_heads, kv_heads)
    return _mm(o, w_o.astype(BF16), b_o, F32, name="swa_out")


def _fox_mixer(hn, w_qkvf, b_f, w_o):
    heads = b_f.shape[0]
    hw = heads * FOX_HEAD_DIM
    q_scale = FOX_HEAD_DIM ** -0.5 * LOG2E
    w_main = jnp.concatenate([(w_qkvf[:, :hw] * q_scale).astype(BF16), w_qkvf[:, hw:3 * hw].astype(BF16)], axis=1)
    proj = _mm(hn, w_main, None, BF16, name="fox_qkv")
    pad = LANES - heads
    w_f = jnp.pad(w_qkvf[:, 3 * hw:], ((0, 0), (0, pad))).astype(BF16)
    logits = _mm(hn, w_f, None, F32, name="fox_gate")
    c, hi, mid, lo = _logsig_cumsum(logits, jnp.pad(b_f.astype(F32), (0, pad)).reshape(1, LANES))
    o = _fox_attention(proj, c, (hi, mid, lo), heads)
    return _mm(o, w_o.astype(BF16), None, F32, name="fox_out")


def _s5_mixer(hn, lam_re, lam_im, log_dt, b_re, b_im, c_re, c_im, d, w_glu, b_glu):
    g = _s5_core(hn, lam_re, lam_im, log_dt, b_re, b_im, c_re, c_im, d)
    return _glu_mm(g, w_glu.astype(BF16), b_glu, F32)


def _conv_ffn(hn, w_in, conv_w, conv_b, w_out):
    g = _ffn_in(hn, w_in.astype(BF16), conv_w.astype(F32), conv_b.astype(F32))
    return _mm(g, w_out.astype(BF16), None, F32, tm_pref=512, tn_pref=512, name="ffn_out")


def _trunk(x, layers):
    b, l, d = x.shape
    assert b == 1
    h = x.reshape(l, d)
    mixers = (_swa_mixer, _fox_mixer, _s5_mixer)
    hn = _prenorm(h, layers[0][0])
    for i, (mix_pre, mix_params, mix_post, ffn_pre, ffn_params, ffn_post) in enumerate(layers):
        y = mixers[i % len(mixers)](hn, *mix_params)
        h, hn = _post_pre(y, h, mix_post, ffn_pre)
        y = _conv_ffn(hn, *ffn_params)
        nxt = layers[i + 1][0] if i + 1 < len(layers) else None
        h, hn = _post_pre(y, h, ffn_post, nxt)
    return h.reshape(b, l, d)


def kernel(x,
           l0_mix_pre_g, l0_swa_w_qkv, l0_swa_b_qkv, l0_swa_sinks, l0_swa_w_o, l0_swa_b_o, l0_mix_post_g,
           l0_ffn_pre_g, l0_ffn_w_in, l0_ffn_conv_w, l0_ffn_conv_b, l0_ffn_w_out, l0_ffn_post_g,
           l1_mix_pre_g, l1_fox_w_qkvf, l1_fox_b_f, l1_fox_w_o, l1_mix_post_g,
           l1_ffn_pre_g, l1_ffn_w_in, l1_ffn_conv_w, l1_ffn_conv_b, l1_ffn_w_out, l1_ffn_post_g,
           l2_mix_pre_g, l2_s5_lambda_re, l2_s5_lambda_im, l2_s5_log_dt, l2_s5_b_re, l2_s5_b_im,
           l2_s5_c_re, l2_s5_c_im, l2_s5_d, l2_s5_w_glu, l2_s5_b_glu, l2_mix_post_g,
           l2_ffn_pre_g, l2_ffn_w_in, l2_ffn_conv_w, l2_ffn_conv_b, l2_ffn_w_out, l2_ffn_post_g,
           l3_mix_pre_g, l3_swa_w_qkv, l3_swa_b_qkv, l3_swa_sinks, l3_swa_w_o, l3_swa_b_o, l3_mix_post_g,
           l3_ffn_pre_g, l3_ffn_w_in, l3_ffn_conv_w, l3_ffn_conv_b, l3_ffn_w_out, l3_ffn_post_g):
    layers = (
        (l0_mix_pre_g, (l0_swa_w_qkv, l0_swa_b_qkv, l0_swa_sinks, l0_swa_w_o, l0_swa_b_o), l0_mix_post_g,
         l0_ffn_pre_g, (l0_ffn_w_in, l0_ffn_conv_w, l0_ffn_conv_b, l0_ffn_w_out), l0_ffn_post_g),
        (l1_mix_pre_g, (l1_fox_w_qkvf, l1_fox_b_f, l1_fox_w_o), l1_mix_post_g,
         l1_ffn_pre_g, (l1_ffn_w_in, l1_ffn_conv_w, l1_ffn_conv_b, l1_ffn_w_out), l1_ffn_post_g),
        (l2_mix_pre_g, (l2_s5_lambda_re, l2_s5_lambda_im, l2_s5_log_dt, l2_s5_b_re, l2_s5_b_im,
                        l2_s5_c_re, l2_s5_c_im, l2_s5_d, l2_s5_w_glu, l2_s5_b_glu), l2_mix_post_g,
         l2_ffn_pre_g, (l2_ffn_w_in, l2_ffn_conv_w, l2_ffn_conv_b, l2_ffn_w_out), l2_ffn_post_g),
        (l3_mix_pre_g, (l3_swa_w_qkv, l3_swa_b_qkv, l3_swa_sinks, l3_swa_w_o, l3_swa_b_o), l3_mix_post_g,
         l3_ffn_pre_g, (l3_ffn_w_in, l3_ffn_conv_w, l3_ffn_conv_b, l3_ffn_w_out), l3_ffn_post_g),
    )
    return _trunk(x, layers)
```

```python
import functools
import math

import jax
import jax.numpy as jnp
from jax import lax
from jax.experimental import pallas as pl
from jax.experimental.pallas import tpu as pltpu

F32 = jnp.float32
BF16 = jnp.bfloat16

NORM_EPS = 1e-6
MASK_VALUE = -1e30
LOG2E = math.log2(math.e)
ATTN_BLOCK = 128
WINDOW = 128
ROPE_THETA = 500000.0
SWA_KV_HEADS_PER_Q = 8
SWA_HEAD_DIM = 64
FOX_HEAD_DIM = 128
S5_GROUP = 16
S5_STATE = 64
S5_MAX_RE = -1e-4
CONV_WIDTH = 3

LANES = 128
SUBLANES = 8
VMEM_LIMIT_BYTES = 56 * 1024 * 1024


def _params(sem, vmem=VMEM_LIMIT_BYTES):
    return pltpu.CompilerParams(dimension_semantics=sem, vmem_limit_bytes=vmem)


def _tile(dim, pref, mult):
    if dim <= pref:
        return dim
    t = (pref // mult) * mult
    while t > mult and dim % t:
        t -= mult
    assert dim % t == 0, (dim, pref, mult)
    return t


def _rms(x, g):
    ms = jnp.mean(x * x, axis=-1, keepdims=True)
    return x * lax.rsqrt(ms + NORM_EPS) * g


def _prenorm_kernel(x_ref, g_ref, o_ref):
    o_ref[...] = _rms(x_ref[...], g_ref[...]).astype(o_ref.dtype)


def _prenorm(x, g):
    m, d = x.shape
    tr = _tile(m, 256, SUBLANES)
    return pl.pallas_call(
        _prenorm_kernel,
        grid=(m // tr,),
        in_specs=[pl.BlockSpec((tr, d), lambda i: (i, 0)), pl.BlockSpec((1, d), lambda i: (0, 0))],
        out_specs=pl.BlockSpec((tr, d), lambda i: (i, 0)),
        out_shape=jax.ShapeDtypeStruct((m, d), BF16),
        compiler_params=_params(("parallel",)),
        name="prenorm",
    )(x, g.reshape(1, d))


def _post_pre_kernel(y_ref, h_ref, gp_ref, gn_ref, h_out_ref, hn_ref):
    h_new = h_ref[...] + _rms(y_ref[...].astype(F32), gp_ref[...])
    h_out_ref[...] = h_new
    hn_ref[...] = _rms(h_new, gn_ref[...]).astype(hn_ref.dtype)


def _post_kernel(y_ref, h_ref, gp_ref, h_out_ref):
    h_out_ref[...] = h_ref[...] + _rms(y_ref[...].astype(F32), gp_ref[...])


def _post_pre(y, h, g_post, g_next):
    m, d = h.shape
    tr = _tile(m, 256, SUBLANES)
    row = pl.BlockSpec((tr, d), lambda i: (i, 0))
    vec = pl.BlockSpec((1, d), lambda i: (0, 0))
    if g_next is None:
        return pl.pallas_call(
            _post_kernel, grid=(m // tr,), in_specs=[row, row, vec], out_specs=row,
            out_shape=jax.ShapeDtypeStruct((m, d), F32),
            compiler_params=_params(("parallel",)), name="post",
        )(y, h, g_post.reshape(1, d)), None
    return pl.pallas_call(
        _post_pre_kernel, grid=(m // tr,), in_specs=[row, row, vec, vec], out_specs=[row, row],
        out_shape=[jax.ShapeDtypeStruct((m, d), F32), jax.ShapeDtypeStruct((m, d), BF16)],
        compiler_params=_params(("parallel",)), name="post_pre",
    )(y, h, g_post.reshape(1, d), g_next.reshape(1, d))


def _mm_kernel(x_ref, w_ref, o_ref):
    o_ref[...] = jnp.dot(x_ref[...], w_ref[...], preferred_element_type=F32).astype(o_ref.dtype)


def _mm_bias_kernel(x_ref, w_ref, b_ref, o_ref):
    acc = jnp.dot(x_ref[...], w_ref[...], preferred_element_type=F32) + b_ref[...]
    o_ref[...] = acc.astype(o_ref.dtype)


def _mm(x, w, b, out_dtype, tm_pref=1024, tn_pref=1024, name="mm"):
    m, k = x.shape
    n = w.shape[1]
    tm = _tile(m, tm_pref, 16)
    tn = _tile(n, tn_pref, LANES)
    in_specs = [pl.BlockSpec((tm, k), lambda i, j: (i, 0)), pl.BlockSpec((k, tn), lambda i, j: (0, j))]
    args = [x, w]
    body = _mm_kernel
    if b is not None:
        in_specs.append(pl.BlockSpec((1, tn), lambda i, j: (0, j)))
        args.append(b.reshape(1, n).astype(F32))
        body = _mm_bias_kernel
    return pl.pallas_call(
        body, grid=(m // tm, n // tn), in_specs=in_specs,
        out_specs=pl.BlockSpec((tm, tn), lambda i, j: (i, j)),
        out_shape=jax.ShapeDtypeStruct((m, n), out_dtype),
        compiler_params=_params(("parallel", "arbitrary")), name=name,
    )(*args)


def _glu_kernel(x_ref, w1_ref, w2_ref, b1_ref, b2_ref, o_ref):
    x = x_ref[...]
    z1 = jnp.dot(x, w1_ref[...], preferred_element_type=F32) + b1_ref[...]
    z2 = jnp.dot(x, w2_ref[...], preferred_element_type=F32) + b2_ref[...]
    o_ref[...] = (z1 * jax.nn.sigmoid(z2)).astype(o_ref.dtype)


def _glu_mm(x, w, b, out_dtype):
    m, k = x.shape
    n = w.shape[1] // 2
    tm = _tile(m, 1024, 16)
    tn = _tile(n, 512, LANES)
    nj = n // tn
    b2 = b.reshape(1, 2 * n).astype(F32)
    return pl.pallas_call(
        _glu_kernel, grid=(m // tm, nj),
        in_specs=[pl.BlockSpec((tm, k), lambda i, j: (i, 0)),
                  pl.BlockSpec((k, tn), lambda i, j: (0, j)),
                  pl.BlockSpec((k, tn), lambda i, j: (0, j + nj)),
                  pl.BlockSpec((1, tn), lambda i, j: (0, j)),
                  pl.BlockSpec((1, tn), lambda i, j: (0, j + nj))],
        out_specs=pl.BlockSpec((tm, tn), lambda i, j: (i, j)),
        out_shape=jax.ShapeDtypeStruct((m, n), out_dtype),
        compiler_params=_params(("parallel", "arbitrary")), name="glu_mm",
    )(x, w, w, b2, b2)


def _ffn_in_kernel(x_ref, wg_ref, wu_ref, cwg_ref, cwu_ref, cbg_ref, cbu_ref, o_ref, carry_g, carry_u,
                   zg_ref, zu_ref, *, sub):
    i = pl.program_id(0)
    j = pl.program_id(1)
    tm = x_ref.shape[0]
    tn = o_ref.shape[1]
    s = SUBLANES
    nt = sub // s

    @pl.when(i == 0)
    def _():
        carry_g[j] = jnp.zeros(carry_g.shape[1:], F32)
        carry_u[j] = jnp.zeros(carry_u.shape[1:], F32)

    sidx = lax.broadcasted_iota(jnp.int32, (nt, s, tn), 1)

    def conv(z, prev, cw_ref, cb_ref):
        z3 = z.reshape(nt, s, tn)

        def shifted(d):
            r = pltpu.roll(z3, d, 1)
            rp = jnp.concatenate([pltpu.roll(prev, d, 0)[None], r[:-1]], axis=0)
            return jnp.where(sidx < d, rp, r)

        cw = cw_ref[...]
        out = cb_ref[...] + cw[2:3, :] * z3 + cw[0:1, :] * shifted(2) + cw[1:2, :] * shifted(1)
        return out.reshape(sub, tn), z3[nt - 1]

    wg = wg_ref[...]
    wu = wu_ref[...]
    prev_g = carry_g[j]
    prev_u = carry_u[j]
    zero = jnp.minimum(j, 0)

    def dots(r):
        x = x_ref[r * sub:(r + 1) * sub, :]
        zg_ref[zero + r % 2] = jnp.dot(x, wg, preferred_element_type=F32)
        zu_ref[zero + r % 2] = jnp.dot(x, wu, preferred_element_type=F32)

    nsub = tm // sub
    dots(0)
    for r in range(nsub):
        if r + 1 < nsub:
            dots(r + 1)
        gate, prev_g = conv(zg_ref[zero + r % 2], prev_g, cwg_ref, cbg_ref)
        up, prev_u = conv(zu_ref[zero + r % 2], prev_u, cwu_ref, cbu_ref)
        o_ref[r * sub:(r + 1) * sub, :] = (jax.nn.gelu(gate, approximate=True) * up).astype(o_ref.dtype)
    carry_g[j] = prev_g
    carry_u[j] = prev_u


def _ffn_in(x, w_in, conv_w, conv_b):
    m, k = x.shape
    dff = w_in.shape[1] // 2
    tm = _tile(m, 2048, 16)
    tn = _tile(dff, 256, LANES)
    nj = dff // tn
    cb = conv_b.reshape(1, 2 * dff)
    sub = _tile(tm, 128, 16)
    kern = functools.partial(_ffn_in_kernel, sub=sub)
    return pl.pallas_call(
        kern, grid=(m // tm, nj),
        in_specs=[pl.BlockSpec((tm, k), lambda i, j: (i, 0)),
                  pl.BlockSpec((k, tn), lambda i, j: (0, j)),
                  pl.BlockSpec((k, tn), lambda i, j: (0, j + nj)),
                  pl.BlockSpec((CONV_WIDTH, tn), lambda i, j: (0, j)),
                  pl.BlockSpec((CONV_WIDTH, tn), lambda i, j: (0, j + nj)),
                  pl.BlockSpec((1, tn), lambda i, j: (0, j)),
                  pl.BlockSpec((1, tn), lambda i, j: (0, j + nj))],
        out_specs=pl.BlockSpec((tm, tn), lambda i, j: (i, j)),
        out_shape=jax.ShapeDtypeStruct((m, dff), BF16),
        scratch_shapes=[pltpu.VMEM((nj, SUBLANES, tn), F32), pltpu.VMEM((nj, SUBLANES, tn), F32),
                        pltpu.VMEM((2, sub, tn), F32), pltpu.VMEM((2, sub, tn), F32)],
        compiler_params=_params(("arbitrary", "arbitrary")), name="ffn_in",
    )(x, w_in, w_in, conv_w, conv_w, cb, cb)


def _swa_kernel(sinks_ref, q_ref, ko_ref, kp_ref, vo_ref, vp_ref, ro_ref, rp_ref, bias_ref, o_ref, *,
                group, scale):
    n = pl.program_id(0)
    hp = pl.program_id(1)
    blk = q_ref.shape[0]
    half = LANES // 2
    nk = 2 * blk
    gw = group * half

    def rope(x, r):
        c, s1, s2 = r[:, 0:LANES], r[:, LANES:2 * LANES], r[:, 2 * LANES:3 * LANES]
        return x * c + pltpu.roll(x, LANES - 8, 1) * s1 + pltpu.roll(x, 8, 1) * s2

    r_own = ro_ref[...]
    r_both = jnp.concatenate([rp_ref[...], r_own], axis=0)
    lane = lax.broadcasted_iota(jnp.int32, (nk, LANES), 1)
    bias = bias_ref[jnp.minimum(n, 1)]
    kv_per_step = ko_ref.shape[1] // half

    def per_head(fn):
        return jnp.concatenate([jnp.broadcast_to(fn(g), (blk, nk)) for g in range(group)], axis=1)

    for kvi in range(kv_per_step):
        kv = kvi % 2
        if kv == 0:
            lanes = slice((kvi // 2) * LANES, (kvi // 2 + 1) * LANES)
            k = rope(jnp.concatenate([kp_ref[:, lanes], ko_ref[:, lanes]], axis=0).astype(F32), r_both)
            v = jnp.concatenate([vp_ref[:, lanes], vo_ref[:, lanes]], axis=0).astype(F32)
        own = (lane < half) if kv == 0 else (lane >= half)
        k_own = jnp.where(own, k, 0.0)
        v_own = jnp.where(own, v, 0.0)
        k_sw = pltpu.roll(k_own, half, 1)
        v_sw = pltpu.roll(v_own, half, 1)
        k_pad = [x.astype(BF16) for x in ((k_own, k_sw) if kv == 0 else (k_sw, k_own))]
        v_pad = [x.astype(BF16) for x in ((v_own, v_sw) if kv == 0 else (v_sw, v_own))]
        col0 = kvi * gw
        tiles = []
        for t in range(group // 2):
            q = rope(q_ref[:, col0 + t * LANES:col0 + (t + 1) * LANES].astype(F32), r_own)
            q = (q * (scale * LOG2E)).astype(BF16)
            for hh in range(2):
                tiles.append(lax.dot_general(q, k_pad[hh], (((1,), (1,)), ((), ())),
                                             preferred_element_type=F32))
        s = jnp.concatenate(tiles, axis=1) + bias
        sinks = [sinks_ref[(kv_per_step * hp + kvi) * group + g] * LOG2E for g in range(group)]
        m = [jnp.maximum(jnp.max(s[:, g * nk:(g + 1) * nk], axis=1, keepdims=True), sinks[g])
             for g in range(group)]
        p = jnp.exp2(s - per_head(lambda g: m[g]))
        inv = per_head(lambda g: 1.0 / (jnp.sum(p[:, g * nk:(g + 1) * nk], axis=1, keepdims=True)
                                        + jnp.exp2(sinks[g] - m[g])))
        p = (p * inv).astype(BF16)
        for t in range(group // 2):
            g0 = 2 * t
            o_tile = (jnp.dot(p[:, g0 * nk:(g0 + 1) * nk], v_pad[0], preferred_element_type=F32)
                      + jnp.dot(p[:, (g0 + 1) * nk:(g0 + 2) * nk], v_pad[1], preferred_element_type=F32))
            o_ref[:, col0 + t * LANES:col0 + (t + 1) * LANES] = o_tile.astype(o_ref.dtype)


def _rope_tables(l, head_dim):
    rope_dim = head_dim // 4
    hf = rope_dim // 2
    inv_freq = jnp.exp(-math.log(ROPE_THETA) * jnp.arange(hf, dtype=F32) * (2.0 / rope_dim))
    ang = jnp.arange(l).astype(F32)[:, None] * inv_freq[None, :]
    cos, sin = jnp.cos(ang), jnp.sin(ang)
    rest = head_dim - rope_dim
    c = jnp.concatenate([cos, cos, jnp.ones((l, rest), F32)], axis=1)
    s1 = jnp.concatenate([-sin, jnp.zeros((l, head_dim - hf), F32)], axis=1)
    s2 = jnp.concatenate([jnp.zeros((l, hf), F32), sin, jnp.zeros((l, rest), F32)], axis=1)
    reps = LANES // head_dim
    return jnp.concatenate([jnp.tile(c, (1, reps)), jnp.tile(s1, (1, reps)), jnp.tile(s2, (1, reps))], axis=1)


def _swa_attention(qkv, sinks, q_heads, kv_heads):
    l = qkv.shape[0]
    dh = SWA_HEAD_DIM
    group = q_heads // kv_heads
    qw = q_heads * dh
    kvw = kv_heads * dh
    assert dh * 2 == LANES and group % 2 == 0 and kv_heads % 2 == 0
    blk = ATTN_BLOCK
    nb = l // blk
    gw = group * dh
    kps = 2
    kw = kps * dh
    k_col0 = qw // kw
    v_col0 = (qw + kvw) // kw
    rope = _rope_tables(l, dh)
    nk = 2 * blk
    qpos = jnp.arange(blk)[:, None]
    kj = jnp.arange(group * nk)[None, :] % nk
    rel = qpos + blk - kj
    band = (rel >= 0) & (rel < WINDOW)
    bias = jnp.where(jnp.stack([band & (kj >= blk), band]), 0.0, MASK_VALUE).astype(F32)
    prev = lambda n: jnp.maximum(n - 1, 0)
    kern = functools.partial(_swa_kernel, group=group, scale=dh ** -0.5)
    return pl.pallas_call(
        kern, grid=(nb, kv_heads // kps),
        in_specs=[pl.BlockSpec(memory_space=pltpu.SMEM),
                  pl.BlockSpec((blk, kps * gw), lambda n, h: (n, h)),
                  pl.BlockSpec((blk, kw), lambda n, h: (n, k_col0 + h)),
                  pl.BlockSpec((blk, kw), lambda n, h: (prev(n), k_col0 + h)),
                  pl.BlockSpec((blk, kw), lambda n, h: (n, v_col0 + h)),
                  pl.BlockSpec((blk, kw), lambda n, h: (prev(n), v_col0 + h)),
                  pl.BlockSpec((blk, 3 * LANES), lambda n, h: (n, 0)),
                  pl.BlockSpec((blk, 3 * LANES), lambda n, h: (prev(n), 0)),
                  pl.BlockSpec((2, blk, group * nk), lambda n, h: (0, 0, 0))],
        out_specs=pl.BlockSpec((blk, kps * gw), lambda n, h: (n, h)),
        out_shape=jax.ShapeDtypeStruct((l, qw), BF16),
        compiler_params=_params(("parallel", "arbitrary")), name="swa_attn",
    )(sinks.astype(F32), qkv, qkv, qkv, qkv, qkv, rope, rope, bias)


def _logsig_cumsum_kernel(x_ref, b_ref, c_ref, hi_ref, mid_ref, lo_ref, carry_ref):
    i = pl.program_id(0)

    @pl.when(i == 0)
    def _():
        carry_ref[...] = jnp.zeros(carry_ref.shape, F32)

    x = x_ref[...] + b_ref[...]
    lf = jnp.minimum(x, 0.0) - jnp.log1p(jnp.exp(-jnp.abs(x)))
    rows = lf.shape[0]
    row = lax.broadcasted_iota(jnp.int32, lf.shape, 0)
    d = 1
    while d < rows:
        lf = lf + jnp.where(row >= d, pltpu.roll(lf, d, 0), 0.0)
        d *= 2
    lf = lf + carry_ref[0:1, :]
    carry_ref[...] = jnp.broadcast_to(lf[rows - 1:rows, :], carry_ref.shape)
    c = lf * LOG2E
    c_ref[...] = c
    neg = -c
    hi = neg.astype(BF16)
    rem = neg - hi.astype(F32)
    mid = rem.astype(BF16)
    hi_ref[...] = hi
    mid_ref[...] = mid
    lo_ref[...] = (rem - mid.astype(F32)).astype(BF16)


def _logsig_cumsum(x, b):
    l, w = x.shape
    tc = _tile(l, 512, 16)
    blk = pl.BlockSpec((tc, w), lambda i: (i, 0))
    return pl.pallas_call(
        _logsig_cumsum_kernel, grid=(l // tc,),
        in_specs=[blk, pl.BlockSpec((1, w), lambda i: (0, 0))],
        out_specs=[blk, blk, blk, blk],
        out_shape=[jax.ShapeDtypeStruct((l, w), F32)] + [jax.ShapeDtypeStruct((l, w), BF16)] * 3,
        scratch_shapes=[pltpu.VMEM((SUBLANES, w), F32)],
        compiler_params=_params(("arbitrary",)), name="logsig_cumsum",
    )(x, b)


def _fox_kernel(qt_ref, k_ref, a_ref, vt_ref, cq_ref, o_ref, m_ref, l_ref, acc_ref, t_ref, p_ref, *, tq, tk):
    h = pl.program_id(0)
    qi = pl.program_id(1)
    row = lax.broadcasted_iota(jnp.int32, (LANES, tq), 0)
    sel = jnp.where((row >= 3 * h) & (row < 3 * h + 3), 1.0, 0.0).astype(BF16)
    qa = jnp.concatenate([qt_ref[...], sel], axis=0)
    cq = cq_ref[...]
    m_ref[...] = jnp.full(m_ref.shape, MASK_VALUE, F32)
    l_ref[...] = jnp.zeros(l_ref.shape, F32)
    acc_ref[...] = jnp.zeros(acc_ref.shape, F32)
    p_ref[1] = jnp.zeros(p_ref.shape[1:], BF16)
    zero = jnp.minimum(qi, 0)

    def logits(j, slot):
        r0 = pl.multiple_of(j * tk, tk)
        ka = jnp.concatenate([k_ref[pl.ds(r0, tk), :], a_ref[pl.ds(r0, tk), :]], axis=1)
        t_ref[zero + slot] = jnp.dot(ka, qa, preferred_element_type=F32)

    def pv(j, slot):
        return jnp.dot(vt_ref[j], p_ref[zero + slot], preferred_element_type=F32)

    def softmax(j, slot, masked, pv_prev):
        t = t_ref[zero + slot]
        if masked:
            kpos = j * tk + lax.broadcasted_iota(jnp.int32, t.shape, 0)
            qpos = qi * tq + lax.broadcasted_iota(jnp.int32, t.shape, 1)
            t = jnp.where(kpos <= qpos, t, MASK_VALUE)
        m_old = m_ref[...]
        m_new = jnp.maximum(m_old, jnp.max(t, axis=0, keepdims=True) + cq)
        alpha = jnp.exp2(m_old - m_new)
        p = jnp.exp2(t - (m_new - cq))
        l_ref[...] = alpha * l_ref[...] + jnp.sum(p, axis=0, keepdims=True)
        p_ref[zero + slot] = p.astype(BF16)
        acc_ref[...] = alpha * (acc_ref[...] + pv_prev)
        m_ref[...] = m_new

    def pair(a, masked):
        pv_prev = pv(jnp.maximum(a - 1, 0), 1)
        logits(a + 1, 1)
        softmax(a, 0, masked, pv_prev)
        pv_prev = pv(a, 0)
        if not masked:
            logits(a + 2, 0)
        softmax(a + 1, 1, masked, pv_prev)

    assert tq == 2 * tk
    logits(0, 0)

    def full_body(it, c):
        pair(2 * it, False)
        return c

    lax.fori_loop(0, qi, full_body, 0)
    pair(2 * qi, True)
    acc = acc_ref[...] + pv(2 * qi + 1, 1)
    o_ref[...] = (acc / l_ref[...]).T.astype(o_ref.dtype)


def _fox_attention(proj, c, pieces, heads):
    l = proj.shape[0]
    dh = FOX_HEAD_DIM
    hw = heads * dh
    assert dh == LANES and 3 * heads <= LANES
    tq = _tile(l, 1024, LANES)
    tk = _tile(tq, 512, LANES)
    qt = proj[:, :hw].T
    vt = proj[:, 2 * hw:].reshape(l // tk, tk, heads, dh).transpose(2, 0, 3, 1)
    a = jnp.stack([x[:, :heads] for x in pieces], axis=-1).reshape(l, 3 * heads)
    a = jnp.pad(a, ((0, 0), (0, LANES - 3 * heads)))
    cq = c[:, :heads].T.reshape(heads, 1, l)
    kern = functools.partial(_fox_kernel, tq=tq, tk=tk)
    return pl.pallas_call(
        kern, grid=(heads, l // tq),
        in_specs=[pl.BlockSpec((dh, tq), lambda h, i: (h, i)),
                  pl.BlockSpec((l, dh), lambda h, i: (0, heads + h)),
                  pl.BlockSpec((l, LANES), lambda h, i: (0, 0)),
                  pl.BlockSpec((None, l // tk, dh, tk), lambda h, i: (h, 0, 0, 0)),
                  pl.BlockSpec((None, 1, tq), lambda h, i: (h, 0, i))],
        out_specs=pl.BlockSpec((tq, dh), lambda h, i: (i, h)),
        out_shape=jax.ShapeDtypeStruct((l, hw), BF16),
        scratch_shapes=[pltpu.VMEM((1, tq), F32), pltpu.VMEM((1, tq), F32), pltpu.VMEM((dh, tq), F32),
                        pltpu.VMEM((2, tk, tq), F32), pltpu.VMEM((2, tk, tq), BF16)],
        compiler_params=_params(("parallel", "arbitrary")), name="fox_attn",
    )(qt, proj, a, vt, cq)


S5_GROUPS_PER_BLOCK = 16


def _s5_kernel(u_ref, b_ref, tab_ref, c_ref, d_ref, o_ref, xs_ref, xb_ref, carry_ref, *, ns, sub):
    c = pl.program_id(1)

    @pl.when(c == 0)
    def _():
        carry_ref[...] = jnp.zeros(carry_ref.shape, F32)

    t = u_ref.shape[0]
    s = SUBLANES
    nsub = t // sub
    zero = jnp.minimum(c, 0)

    def tab(k):
        return tab_ref[k * s:(k + 1) * s, :]

    def project_in(r):
        xs_ref[zero + r % 2] = jnp.dot(u_ref[r * sub:(r + 1) * sub, :], b_ref[...], preferred_element_type=F32)

    def scan(r, cr, ci):
        slot = zero + r % 2
        for k in range(sub // s):
            vr = xs_ref[slot, k * s:(k + 1) * s, 0:ns]
            vi = xs_ref[slot, k * s:(k + 1) * s, ns:2 * ns]
            for idx, d in enumerate((1, 2, 4)):
                ar, ai = tab(2 * idx), tab(2 * idx + 1)
                sr, si = pltpu.roll(vr, d, 0), pltpu.roll(vi, d, 0)
                vr, vi = vr + ar * sr - ai * si, vi + ar * si + ai * sr
            pr, pi = tab(6), tab(7)
            vr, vi = vr + pr * cr - pi * ci, vi + pr * ci + pi * cr
            xb_ref[slot, k * s:(k + 1) * s, 0:ns] = vr
            xb_ref[slot, k * s:(k + 1) * s, ns:2 * ns] = vi
            cr, ci = vr[s - 1:s, :], vi[s - 1:s, :]
        return cr, ci

    def project_out(r):
        x = xb_ref[zero + r % 2].astype(BF16)
        y = jnp.dot(x, c_ref[...], preferred_element_type=F32)
        y = y + d_ref[...] * u_ref[r * sub:(r + 1) * sub, :].astype(F32)
        o_ref[r * sub:(r + 1) * sub, :] = jax.nn.gelu(y, approximate=True).astype(o_ref.dtype)

    cr, ci = carry_ref[0:1, 0:ns], carry_ref[0:1, ns:2 * ns]
    project_in(0)
    for r in range(nsub):
        if r + 1 < nsub:
            project_in(r + 1)
        cr, ci = scan(r, cr, ci)
        if r >= 1:
            project_out(r - 1)
    project_out(nsub - 1)
    carry_ref[0:1, 0:ns] = cr
    carry_ref[0:1, ns:2 * ns] = ci


def _s5_tables(lam_re, lam_im, log_dt, b_re, b_im, c_re, c_im):
    g, p = lam_re.shape
    hch = b_re.shape[2]
    gb = S5_GROUPS_PER_BLOCK
    nb = g // gb
    lre = jnp.minimum(lam_re.astype(F32), S5_MAX_RE)
    lim = lam_im.astype(F32)
    lam = lax.complex(lre, lim)
    dt = jnp.exp(log_dt.astype(F32))[:, None]
    lam_bar = jnp.exp(lam * dt)
    b_bar = ((lam_bar - 1.0) / lam)[..., None] * lax.complex(b_re.astype(F32), b_im.astype(F32))
    eye = jnp.eye(gb, dtype=F32)

    def in_map(x):
        return jnp.einsum('gaph,ab->gahbp', x.reshape(nb, gb, p, hch), eye).reshape(nb, gb * hch, gb * p)

    def out_map(x):
        return jnp.einsum('gahp,ab->gapbh', x.reshape(nb, gb, hch, p), eye).reshape(nb, gb * p, gb * hch)

    b_cat = jnp.concatenate([in_map(jnp.real(b_bar)), in_map(jnp.imag(b_bar))], axis=2).astype(BF16)
    c_cat = jnp.concatenate([out_map(c_re.astype(F32)), out_map(-c_im.astype(F32))], axis=1).astype(BF16)

    def power(k):
        mag = jnp.exp(k * lre * dt)
        ang = k * lim * dt
        return ((mag * jnp.cos(ang)).reshape(nb, 1, gb * p), (mag * jnp.sin(ang)).reshape(nb, 1, gb * p))

    row = jnp.arange(SUBLANES)[None, :, None]
    parts = []
    for d in (1, 2, 4):
        re, im = power(float(d))
        parts += [jnp.where(row >= d, re, 0.0), jnp.where(row >= d, im, 0.0)]
    pre = jnp.concatenate([power(float(r + 1))[0] for r in range(SUBLANES)], axis=1)
    pim = jnp.concatenate([power(float(r + 1))[1] for r in range(SUBLANES)], axis=1)
    parts += [pre, pim]
    tab = jnp.concatenate(parts, axis=1)
    return b_cat, c_cat, tab


def _s5_core(hn, lam_re, lam_im, log_dt, b_re, b_im, c_re, c_im, d):
    l, dm = hn.shape
    gb = S5_GROUPS_PER_BLOCK
    b_cat, c_cat, tab = _s5_tables(lam_re, lam_im, log_dt, b_re, b_im, c_re, c_im)
    nb = b_cat.shape[0]
    wu = gb * S5_GROUP
    ns = gb * S5_STATE
    t = _tile(l, 512, 16)
    sub = _tile(t, 128, 16)
    kern = functools.partial(_s5_kernel, ns=ns, sub=sub)
    return pl.pallas_call(
        kern, grid=(nb, l // t),
        in_specs=[pl.BlockSpec((t, wu), lambda g, c: (c, g)),
                  pl.BlockSpec((None, wu, 2 * ns), lambda g, c: (g, 0, 0)),
                  pl.BlockSpec((None, 8 * SUBLANES, ns), lambda g, c: (g, 0, 0)),
                  pl.BlockSpec((None, 2 * ns, wu), lambda g, c: (g, 0, 0)),
                  pl.BlockSpec((1, wu), lambda g, c: (0, g))],
        out_specs=pl.BlockSpec((t, wu), lambda g, c: (c, g)),
        out_shape=jax.ShapeDtypeStruct((l, dm), BF16),
        scratch_shapes=[pltpu.VMEM((2, sub, 2 * ns), F32), pltpu.VMEM((2, sub, 2 * ns), F32),
                        pltpu.VMEM((SUBLANES, 2 * ns), F32)],
        compiler_params=_params(("parallel", "arbitrary")), name="s5_core",
    )(hn, b_cat, tab, c_cat, d.reshape(1, dm).astype(F32))


def _swa_mixer(hn, w_qkv, b_qkv, sinks, w_o, b_o):
    q_heads = sinks.shape[0]
    kv_heads = q_heads // SWA_KV_HEADS_PER_Q
    qkv = _mm(hn, w_qkv.astype(BF16), b_qkv, BF16, tn_pref=512, name="swa_qkv")
    o = _swa_attention(qkv, sinks, q_heads, kv_heads)
    return _mm(o, w_o.astype(BF16), b_o, BF16, name="swa_out")


def _fox_mixer(hn, w_qkvf, b_f, w_o):
    heads = b_f.shape[0]
    hw = heads * FOX_HEAD_DIM
    q_scale = FOX_HEAD_DIM ** -0.5 * LOG2E
    w_main = jnp.concatenate([(w_qkvf[:, :hw] * q_scale).astype(BF16), w_qkvf[:, hw:3 * hw].astype(BF16)], axis=1)
    proj = _mm(hn, w_main, None, BF16, name="fox_qkv")
    pad = LANES - heads
    w_f = jnp.pad(w_qkvf[:, 3 * hw:], ((0, 0), (0, pad))).astype(BF16)
    logits = _mm(hn, w_f, None, F32, name="fox_gate")
    c, hi, mid, lo = _logsig_cumsum(logits, jnp.pad(b_f.astype(F32), (0, pad)).reshape(1, LANES))
    o = _fox_attention(proj, c, (hi, mid, lo), heads)
    return _mm(o, w_o.astype(BF16), None, BF16, name="fox_out")


def _s5_mixer(hn, lam_re, lam_im, log_dt, b_re, b_im, c_re, c_im, d, w_glu, b_glu):
    g = _s5_core(hn, lam_re, lam_im, log_dt, b_re, b_im, c_re, c_im, d)
    return _glu_mm(g, w_glu.astype(BF16), b_glu, BF16)


def _conv_ffn(hn, w_in, conv_w, conv_b, w_out):
    g = _ffn_in(hn, w_in.astype(BF16), conv_w.astype(F32), conv_b.astype(F32))
    return _mm(g, w_out.astype(BF16), None, BF16, tm_pref=512, tn_pref=512, name="ffn_out")


def _trunk(x, layers):
    b, l, d = x.shape
    assert b == 1
    h = x.reshape(l, d)
    mixers = (_swa_mixer, _fox_mixer, _s5_mixer)
    hn = _prenorm(h, layers[0][0])
    for i, (mix_pre, mix_params, mix_post, ffn_pre, ffn_params, ffn_post) in enumerate(layers):
        y = mixers[i % len(mixers)](hn, *mix_params)
        h, hn = _post_pre(y, h, mix_post, ffn_pre)
        y = _conv_ffn(hn, *ffn_params)
        nxt = layers[i + 1][0] if i + 1 < len(layers) else None
        h, hn = _post_pre(y, h, ffn_post, nxt)
    return h.reshape(b, l, d)


def kernel(x,
           l0_mix_pre_g, l0_swa_w_qkv, l0_swa_b_qkv, l0_swa_sinks, l0_swa_w_o, l0_swa_b_o, l0_mix_post_g,
           l0_ffn_pre_g, l0_ffn_w_in, l0_ffn_conv_w, l0_ffn_conv_b, l0_ffn_w_out, l0_ffn_post_g,
           l1_mix_pre_g, l1_fox_w_qkvf, l1_fox_b_f, l1_fox_w_o, l1_mix_post_g,
           l1_ffn_pre_g, l1_ffn_w_in, l1_ffn_conv_w, l1_ffn_conv_b, l1_ffn_w_out, l1_ffn_post_g,
           l2_mix_pre_g, l2_s5_lambda_re, l2_s5_lambda_im, l2_s5_log_dt, l2_s5_b_re, l2_s5_b_im,
           l2_s5_c_re, l2_s5_c_im, l2_s5_d, l2_s5_w_glu, l2_s5_b_glu, l2_mix_post_g,
           l2_ffn_pre_g, l2_ffn_w_in, l2_ffn_conv_w, l2_ffn_conv_b, l2_ffn_w_out, l2_ffn_post_g,
           l3_mix_pre_g, l3_swa_w_qkv, l3_swa_b_qkv, l3_swa_sinks, l3_swa_w_o, l3_swa_b_o, l3_mix_post_g,
           l3_ffn_pre_g, l3_ffn_w_in, l3_ffn_conv_w, l3_ffn_conv_b, l3_ffn_w_out, l3_ffn_post_g):
    layers = (
        (l0_mix_pre_g, (l0_swa_w_qkv, l0_swa_b_qkv, l0_swa_sinks, l0_swa_w_o, l0_swa_b_o), l0_mix_post_g,
         l0_ffn_pre_g, (l0_ffn_w_in, l0_ffn_conv_w, l0_ffn_conv_b, l0_ffn_w_out), l0_ffn_post_g),
        (l1_mix_pre_g, (l1_fox_w_qkvf, l1_fox_b_f, l1_fox_w_o), l1_mix_post_g,
         l1_ffn_pre_g, (l1_ffn_w_in, l1_ffn_conv_w, l1_ffn_conv_b, l1_ffn_w_out), l1_ffn_post_g),
        (l2_mix_pre_g, (l2_s5_lambda_re, l2_s5_lambda_im, l2_s5_log_dt, l2_s5_b_re, l2_s5_b_im,
                        l2_s5_c_re, l2_s5_c_im, l2_s5_d, l2_s5_w_glu, l2_s5_b_glu), l2_mix_post_g,
         l2_ffn_pre_g, (l2_ffn_w_in, l2_ffn_conv_w, l2_ffn_conv_b, l2_ffn_w_out), l2_ffn_post_g),
        (l3_mix_pre_g, (l3_swa_w_qkv, l3_swa_b_qkv, l3_swa_sinks, l3_swa_w_o, l3_swa_b_o), l3_mix_post_g,
         l3_ffn_pre_g, (l3_ffn_w_in, l3_ffn_conv_w, l3_ffn_conv_b, l3_ffn_w_out), l3_ffn_post_g),
    )
    return _trunk(x, layers)
```

```python
import functools
import math

import jax
import jax.numpy as jnp
from jax import lax
from jax.experimental import pallas as pl
from jax.experimental.pallas import tpu as pltpu

F32 = jnp.float32
BF16 = jnp.bfloat16

NORM_EPS = 1e-6
MASK_VALUE = -1e30
LOG2E = math.log2(math.e)
ATTN_BLOCK = 128
WINDOW = 128
ROPE_THETA = 500000.0
SWA_KV_HEADS_PER_Q = 8
SWA_HEAD_DIM = 64
FOX_HEAD_DIM = 128
S5_GROUP = 16
S5_STATE = 64
S5_MAX_RE = -1e-4
CONV_WIDTH = 3

LANES = 128
SUBLANES = 8
VMEM_LIMIT_BYTES = 56 * 1024 * 1024


def _params(sem, vmem=VMEM_LIMIT_BYTES):
    return pltpu.CompilerParams(dimension_semantics=sem, vmem_limit_bytes=vmem)


def _tile(dim, pref, mult):
    if dim <= pref:
        return dim
    t = (pref // mult) * mult
    while t > mult and dim % t:
        t -= mult
    assert dim % t == 0, (dim, pref, mult)
    return t


def _rms(x, g):
    ms = jnp.mean(x * x, axis=-1, keepdims=True)
    return x * lax.rsqrt(ms + NORM_EPS) * g


def _prenorm_kernel(x_ref, g_ref, o_ref):
    o_ref[...] = _rms(x_ref[...], g_ref[...]).astype(o_ref.dtype)


def _prenorm(x, g):
    m, d = x.shape
    tr = _tile(m, 256, SUBLANES)
    return pl.pallas_call(
        _prenorm_kernel,
        grid=(m // tr,),
        in_specs=[pl.BlockSpec((tr, d), lambda i: (i, 0)), pl.BlockSpec((1, d), lambda i: (0, 0))],
        out_specs=pl.BlockSpec((tr, d), lambda i: (i, 0)),
        out_shape=jax.ShapeDtypeStruct((m, d), BF16),
        compiler_params=_params(("parallel",)),
        name="prenorm",
    )(x, g.reshape(1, d))


def _post_pre_kernel(y_ref, h_ref, gp_ref, gn_ref, h_out_ref, hn_ref):
    h_new = h_ref[...] + _rms(y_ref[...].astype(F32), gp_ref[...])
    h_out_ref[...] = h_new
    hn_ref[...] = _rms(h_new, gn_ref[...]).astype(hn_ref.dtype)


def _post_kernel(y_ref, h_ref, gp_ref, h_out_ref):
    h_out_ref[...] = h_ref[...] + _rms(y_ref[...].astype(F32), gp_ref[...])


def _post_pre(y, h, g_post, g_next):
    m, d = h.shape
    tr = _tile(m, 256, SUBLANES)
    row = pl.BlockSpec((tr, d), lambda i: (i, 0))
    vec = pl.BlockSpec((1, d), lambda i: (0, 0))
    if g_next is None:
        return pl.pallas_call(
            _post_kernel, grid=(m // tr,), in_specs=[row, row, vec], out_specs=row,
            out_shape=jax.ShapeDtypeStruct((m, d), F32),
            compiler_params=_params(("parallel",)), name="post",
        )(y, h, g_post.reshape(1, d)), None
    return pl.pallas_call(
        _post_pre_kernel, grid=(m // tr,), in_specs=[row, row, vec, vec], out_specs=[row, row],
        out_shape=[jax.ShapeDtypeStruct((m, d), F32), jax.ShapeDtypeStruct((m, d), BF16)],
        compiler_params=_params(("parallel",)), name="post_pre",
    )(y, h, g_post.reshape(1, d), g_next.reshape(1, d))


def _mm_kernel(x_ref, w_ref, *rest, has_scale, has_bias):
    o_ref = rest[-1]
    acc = jnp.dot(x_ref[...], w_ref[...].astype(BF16), preferred_element_type=F32)
    if has_scale:
        acc = acc * rest[0][...]
    if has_bias:
        acc = acc + rest[-2][...]
    o_ref[...] = acc.astype(o_ref.dtype)


def _mm(x, w, b, out_dtype, tm_pref=1024, tn_pref=1024, name="mm", n=None, col_scale=None):
    m, k = x.shape
    n = w.shape[1] if n is None else n
    tm = _tile(m, tm_pref, 16)
    tn = _tile(n, tn_pref, LANES)
    vec = pl.BlockSpec((1, tn), lambda i, j: (0, j))
    in_specs = [pl.BlockSpec((tm, k), lambda i, j: (i, 0)), pl.BlockSpec((k, tn), lambda i, j: (0, j))]
    args = [x, w]
    for v in (col_scale, b):
        if v is not None:
            in_specs.append(vec)
            args.append(v.reshape(1, n).astype(F32))
    body = functools.partial(_mm_kernel, has_scale=col_scale is not None, has_bias=b is not None)
    return pl.pallas_call(
        body, grid=(m // tm, n // tn), in_specs=in_specs,
        out_specs=pl.BlockSpec((tm, tn), lambda i, j: (i, j)),
        out_shape=jax.ShapeDtypeStruct((m, n), out_dtype),
        compiler_params=_params(("parallel", "arbitrary")), name=name,
    )(*args)


def _glu_kernel(x_ref, w1_ref, w2_ref, b1_ref, b2_ref, o_ref):
    x = x_ref[...]
    z1 = jnp.dot(x, w1_ref[...], preferred_element_type=F32) + b1_ref[...]
    z2 = jnp.dot(x, w2_ref[...], preferred_element_type=F32) + b2_ref[...]
    o_ref[...] = (z1 * jax.nn.sigmoid(z2)).astype(o_ref.dtype)


def _glu_mm(x, w, b, out_dtype):
    m, k = x.shape
    n = w.shape[1] // 2
    tm = _tile(m, 1024, 16)
    tn = _tile(n, 512, LANES)
    nj = n // tn
    b2 = b.reshape(1, 2 * n).astype(F32)
    return pl.pallas_call(
        _glu_kernel, grid=(m // tm, nj),
        in_specs=[pl.BlockSpec((tm, k), lambda i, j: (i, 0)),
                  pl.BlockSpec((k, tn), lambda i, j: (0, j)),
                  pl.BlockSpec((k, tn), lambda i, j: (0, j + nj)),
                  pl.BlockSpec((1, tn), lambda i, j: (0, j)),
                  pl.BlockSpec((1, tn), lambda i, j: (0, j + nj))],
        out_specs=pl.BlockSpec((tm, tn), lambda i, j: (i, j)),
        out_shape=jax.ShapeDtypeStruct((m, n), out_dtype),
        compiler_params=_params(("parallel", "arbitrary")), name="glu_mm",
    )(x, w, w, b2, b2)


def _ffn_in_kernel(x_ref, wg_ref, wu_ref, cwg_ref, cwu_ref, cbg_ref, cbu_ref, o_ref, carry_g, carry_u,
                   zg_ref, zu_ref, *, sub):
    i = pl.program_id(0)
    j = pl.program_id(1)
    tm = x_ref.shape[0]
    tn = o_ref.shape[1]
    s = SUBLANES
    nt = sub // s

    @pl.when(i == 0)
    def _():
        carry_g[j] = jnp.zeros(carry_g.shape[1:], F32)
        carry_u[j] = jnp.zeros(carry_u.shape[1:], F32)

    sidx = lax.broadcasted_iota(jnp.int32, (nt, s, tn), 1)

    def conv(z, prev, cw_ref, cb_ref):
        z3 = z.reshape(nt, s, tn)

        def shifted(d):
            r = pltpu.roll(z3, d, 1)
            rp = jnp.concatenate([pltpu.roll(prev, d, 0)[None], r[:-1]], axis=0)
            return jnp.where(sidx < d, rp, r)

        cw = cw_ref[...]
        out = cb_ref[...] + cw[2:3, :] * z3 + cw[0:1, :] * shifted(2) + cw[1:2, :] * shifted(1)
        return out.reshape(sub, tn), z3[nt - 1]

    wg = wg_ref[...].astype(BF16)
    wu = wu_ref[...].astype(BF16)
    prev_g = carry_g[j]
    prev_u = carry_u[j]
    zero = jnp.minimum(j, 0)

    def dots(r):
        x = x_ref[r * sub:(r + 1) * sub, :]
        zg_ref[zero + r % 2] = jnp.dot(x, wg, preferred_element_type=F32)
        zu_ref[zero + r % 2] = jnp.dot(x, wu, preferred_element_type=F32)

    nsub = tm // sub
    dots(0)
    for r in range(nsub):
        if r + 1 < nsub:
            dots(r + 1)
        gate, prev_g = conv(zg_ref[zero + r % 2], prev_g, cwg_ref, cbg_ref)
        up, prev_u = conv(zu_ref[zero + r % 2], prev_u, cwu_ref, cbu_ref)
        o_ref[r * sub:(r + 1) * sub, :] = (jax.nn.gelu(gate, approximate=True) * up).astype(o_ref.dtype)
    carry_g[j] = prev_g
    carry_u[j] = prev_u


def _ffn_in(x, w_in, conv_w, conv_b):
    m, k = x.shape
    dff = w_in.shape[1] // 2
    tm = _tile(m, 2048, 16)
    tn = _tile(dff, 256, LANES)
    nj = dff // tn
    cb = conv_b.reshape(1, 2 * dff)
    sub = _tile(tm, 128, 16)
    kern = functools.partial(_ffn_in_kernel, sub=sub)
    return pl.pallas_call(
        kern, grid=(m // tm, nj),
        in_specs=[pl.BlockSpec((tm, k), lambda i, j: (i, 0)),
                  pl.BlockSpec((k, tn), lambda i, j: (0, j)),
                  pl.BlockSpec((k, tn), lambda i, j: (0, j + nj)),
                  pl.BlockSpec((CONV_WIDTH, tn), lambda i, j: (0, j)),
                  pl.BlockSpec((CONV_WIDTH, tn), lambda i, j: (0, j + nj)),
                  pl.BlockSpec((1, tn), lambda i, j: (0, j)),
                  pl.BlockSpec((1, tn), lambda i, j: (0, j + nj))],
        out_specs=pl.BlockSpec((tm, tn), lambda i, j: (i, j)),
        out_shape=jax.ShapeDtypeStruct((m, dff), BF16),
        scratch_shapes=[pltpu.VMEM((nj, SUBLANES, tn), F32), pltpu.VMEM((nj, SUBLANES, tn), F32),
                        pltpu.VMEM((2, sub, tn), F32), pltpu.VMEM((2, sub, tn), F32)],
        compiler_params=_params(("arbitrary", "arbitrary")), name="ffn_in",
    )(x, w_in, w_in, conv_w, conv_w, cb, cb)


def _swa_kernel(sinks_ref, q_ref, ko_ref, kp_ref, vo_ref, vp_ref, ro_ref, rp_ref, bias_ref, o_ref, *,
                group, scale):
    n = pl.program_id(0)
    hp = pl.program_id(1)
    blk = q_ref.shape[0]
    half = LANES // 2
    nk = 2 * blk
    gw = group * half

    def rope(x, r):
        c, s1, s2 = r[:, 0:LANES], r[:, LANES:2 * LANES], r[:, 2 * LANES:3 * LANES]
        return x * c + pltpu.roll(x, LANES - 8, 1) * s1 + pltpu.roll(x, 8, 1) * s2

    r_own = ro_ref[...]
    r_both = jnp.concatenate([rp_ref[...], r_own], axis=0)
    lane = lax.broadcasted_iota(jnp.int32, (nk, LANES), 1)
    bias = bias_ref[jnp.minimum(n, 1)]
    kv_per_step = ko_ref.shape[1] // half

    def per_head(fn):
        return jnp.concatenate([jnp.broadcast_to(fn(g), (blk, nk)) for g in range(group)], axis=1)

    for kvi in range(kv_per_step):
        kv = kvi % 2
        if kv == 0:
            lanes = slice((kvi // 2) * LANES, (kvi // 2 + 1) * LANES)
            k = rope(jnp.concatenate([kp_ref[:, lanes], ko_ref[:, lanes]], axis=0).astype(F32), r_both)
            v = jnp.concatenate([vp_ref[:, lanes], vo_ref[:, lanes]], axis=0).astype(F32)
        own = (lane < half) if kv == 0 else (lane >= half)
        k_own = jnp.where(own, k, 0.0)
        v_own = jnp.where(own, v, 0.0)
        k_sw = pltpu.roll(k_own, half, 1)
        v_sw = pltpu.roll(v_own, half, 1)
        k_pad = [x.astype(BF16) for x in ((k_own, k_sw) if kv == 0 else (k_sw, k_own))]
        v_pad = [x.astype(BF16) for x in ((v_own, v_sw) if kv == 0 else (v_sw, v_own))]
        col0 = kvi * gw
        tiles = []
        for t in range(group // 2):
            q = rope(q_ref[:, col0 + t * LANES:col0 + (t + 1) * LANES].astype(F32), r_own)
            q = (q * (scale * LOG2E)).astype(BF16)
            for hh in range(2):
                tiles.append(lax.dot_general(q, k_pad[hh], (((1,), (1,)), ((), ())),
                                             preferred_element_type=F32))
        s = jnp.concatenate(tiles, axis=1) + bias
        sinks = [sinks_ref[(kv_per_step * hp + kvi) * group + g] * LOG2E for g in range(group)]
        m = [jnp.maximum(jnp.max(s[:, g * nk:(g + 1) * nk], axis=1, keepdims=True), sinks[g])
             for g in range(group)]
        p = jnp.exp2(s - per_head(lambda g: m[g]))
        inv = per_head(lambda g: 1.0 / (jnp.sum(p[:, g * nk:(g + 1) * nk], axis=1, keepdims=True)
                                        + jnp.exp2(sinks[g] - m[g])))
        p = (p * inv).astype(BF16)
        for t in range(group // 2):
            g0 = 2 * t
            o_tile = (jnp.dot(p[:, g0 * nk:(g0 + 1) * nk], v_pad[0], preferred_element_type=F32)
                      + jnp.dot(p[:, (g0 + 1) * nk:(g0 + 2) * nk], v_pad[1], preferred_element_type=F32))
            o_ref[:, col0 + t * LANES:col0 + (t + 1) * LANES] = o_tile.astype(o_ref.dtype)


def _rope_tables(l, head_dim):
    rope_dim = head_dim // 4
    hf = rope_dim // 2
    inv_freq = jnp.exp(-math.log(ROPE_THETA) * jnp.arange(hf, dtype=F32) * (2.0 / rope_dim))
    ang = jnp.arange(l).astype(F32)[:, None] * inv_freq[None, :]
    cos, sin = jnp.cos(ang), jnp.sin(ang)
    rest = head_dim - rope_dim
    c = jnp.concatenate([cos, cos, jnp.ones((l, rest), F32)], axis=1)
    s1 = jnp.concatenate([-sin, jnp.zeros((l, head_dim - hf), F32)], axis=1)
    s2 = jnp.concatenate([jnp.zeros((l, hf), F32), sin, jnp.zeros((l, rest), F32)], axis=1)
    reps = LANES // head_dim
    return jnp.concatenate([jnp.tile(c, (1, reps)), jnp.tile(s1, (1, reps)), jnp.tile(s2, (1, reps))], axis=1)


def _swa_attention(qkv, sinks, q_heads, kv_heads):
    l = qkv.shape[0]
    dh = SWA_HEAD_DIM
    group = q_heads // kv_heads
    qw = q_heads * dh
    kvw = kv_heads * dh
    assert dh * 2 == LANES and group % 2 == 0 and kv_heads % 2 == 0
    blk = ATTN_BLOCK
    nb = l // blk
    gw = group * dh
    kps = 2
    kw = kps * dh
    k_col0 = qw // kw
    v_col0 = (qw + kvw) // kw
    rope = _rope_tables(l, dh)
    nk = 2 * blk
    qpos = jnp.arange(blk)[:, None]
    kj = jnp.arange(group * nk)[None, :] % nk
    rel = qpos + blk - kj
    band = (rel >= 0) & (rel < WINDOW)
    bias = jnp.where(jnp.stack([band & (kj >= blk), band]), 0.0, MASK_VALUE).astype(F32)
    prev = lambda n: jnp.maximum(n - 1, 0)
    kern = functools.partial(_swa_kernel, group=group, scale=dh ** -0.5)
    return pl.pallas_call(
        kern, grid=(nb, kv_heads // kps),
        in_specs=[pl.BlockSpec(memory_space=pltpu.SMEM),
                  pl.BlockSpec((blk, kps * gw), lambda n, h: (n, h)),
                  pl.BlockSpec((blk, kw), lambda n, h: (n, k_col0 + h)),
                  pl.BlockSpec((blk, kw), lambda n, h: (prev(n), k_col0 + h)),
                  pl.BlockSpec((blk, kw), lambda n, h: (n, v_col0 + h)),
                  pl.BlockSpec((blk, kw), lambda n, h: (prev(n), v_col0 + h)),
                  pl.BlockSpec((blk, 3 * LANES), lambda n, h: (n, 0)),
                  pl.BlockSpec((blk, 3 * LANES), lambda n, h: (prev(n), 0)),
                  pl.BlockSpec((2, blk, group * nk), lambda n, h: (0, 0, 0))],
        out_specs=pl.BlockSpec((blk, kps * gw), lambda n, h: (n, h)),
        out_shape=jax.ShapeDtypeStruct((l, qw), BF16),
        compiler_params=_params(("parallel", "arbitrary")), name="swa_attn",
    )(sinks.astype(F32), qkv, qkv, qkv, qkv, qkv, rope, rope, bias)


def _logsig_cumsum_kernel(x_ref, b_ref, c_ref, hi_ref, mid_ref, lo_ref, carry_ref):
    i = pl.program_id(0)

    @pl.when(i == 0)
    def _():
        carry_ref[...] = jnp.zeros(carry_ref.shape, F32)

    x = x_ref[...] + b_ref[...]
    lf = jnp.minimum(x, 0.0) - jnp.log1p(jnp.exp(-jnp.abs(x)))
    rows = lf.shape[0]
    row = lax.broadcasted_iota(jnp.int32, lf.shape, 0)
    d = 1
    while d < rows:
        lf = lf + jnp.where(row >= d, pltpu.roll(lf, d, 0), 0.0)
        d *= 2
    lf = lf + carry_ref[0:1, :]
    carry_ref[...] = jnp.broadcast_to(lf[rows - 1:rows, :], carry_ref.shape)
    c = lf * LOG2E
    c_ref[...] = c
    neg = -c
    hi = neg.astype(BF16)
    rem = neg - hi.astype(F32)
    mid = rem.astype(BF16)
    hi_ref[...] = hi
    mid_ref[...] = mid
    lo_ref[...] = (rem - mid.astype(F32)).astype(BF16)


def _logsig_cumsum(x, b):
    l, w = x.shape
    tc = _tile(l, 512, 16)
    blk = pl.BlockSpec((tc, w), lambda i: (i, 0))
    return pl.pallas_call(
        _logsig_cumsum_kernel, grid=(l // tc,),
        in_specs=[blk, pl.BlockSpec((1, w), lambda i: (0, 0))],
        out_specs=[blk, blk, blk, blk],
        out_shape=[jax.ShapeDtypeStruct((l, w), F32)] + [jax.ShapeDtypeStruct((l, w), BF16)] * 3,
        scratch_shapes=[pltpu.VMEM((SUBLANES, w), F32)],
        compiler_params=_params(("arbitrary",)), name="logsig_cumsum",
    )(x, b)


def _fox_kernel(qt_ref, k_ref, a_ref, vt_ref, cq_ref, o_ref, m_ref, l_ref, acc_ref, t_ref, p_ref, tmax_ref, *,
                tq, tk):
    h = pl.program_id(0)
    qi = pl.program_id(1)
    row = lax.broadcasted_iota(jnp.int32, (LANES, tq), 0)
    sel = jnp.where((row >= 3 * h) & (row < 3 * h + 3), 1.0, 0.0).astype(BF16)
    qa = jnp.concatenate([qt_ref[...], sel], axis=0)
    cq = cq_ref[...]
    m_ref[...] = jnp.full(m_ref.shape, MASK_VALUE, F32)
    l_ref[...] = jnp.zeros(l_ref.shape, F32)
    acc_ref[...] = jnp.zeros(acc_ref.shape, F32)
    p_ref[1] = jnp.zeros(p_ref.shape[1:], BF16)
    zero = jnp.minimum(qi, 0)

    def logits(j, slot):
        r0 = pl.multiple_of(j * tk, tk)
        ka = jnp.concatenate([k_ref[pl.ds(r0, tk), :], a_ref[pl.ds(r0, tk), :]], axis=1)
        t = jnp.dot(ka, qa, preferred_element_type=F32)
        t_ref[zero + slot] = t
        tmax_ref[zero + slot] = jnp.max(t, axis=0, keepdims=True)

    def pv(j, slot):
        return jnp.dot(vt_ref[j], p_ref[zero + slot], preferred_element_type=F32)

    def softmax(j, slot, masked, pv_prev):
        t = t_ref[zero + slot]
        if masked:
            kpos = j * tk + lax.broadcasted_iota(jnp.int32, t.shape, 0)
            qpos = qi * tq + lax.broadcasted_iota(jnp.int32, t.shape, 1)
            t = jnp.where(kpos <= qpos, t, MASK_VALUE)
            tmax = jnp.max(t, axis=0, keepdims=True)
        else:
            tmax = tmax_ref[zero + slot]
        m_old = m_ref[...]
        m_new = jnp.maximum(m_old, tmax + cq)
        alpha = jnp.exp2(m_old - m_new)
        p = jnp.exp2(t - (m_new - cq))
        l_ref[...] = alpha * l_ref[...] + jnp.sum(p, axis=0, keepdims=True)
        p_ref[zero + slot] = p.astype(BF16)
        acc_ref[...] = alpha * (acc_ref[...] + pv_prev)
        m_ref[...] = m_new

    def pair(a, masked):
        pv_prev = pv(jnp.maximum(a - 1, 0), 1)
        logits(a + 1, 1)
        softmax(a, 0, masked, pv_prev)
        pv_prev = pv(a, 0)
        if not masked:
            logits(a + 2, 0)
        softmax(a + 1, 1, masked, pv_prev)

    assert tq == 2 * tk
    logits(0, 0)

    def full_body(it, c):
        pair(2 * it, False)
        return c

    lax.fori_loop(0, qi, full_body, 0)
    pair(2 * qi, True)
    acc = acc_ref[...] + pv(2 * qi + 1, 1)
    o_ref[...] = (acc / l_ref[...]).T.astype(o_ref.dtype)


def _fox_attention(proj, c, pieces, heads):
    l = proj.shape[0]
    dh = FOX_HEAD_DIM
    hw = heads * dh
    assert dh == LANES and 3 * heads <= LANES
    tq = _tile(l, 1024, LANES)
    tk = _tile(tq, 512, LANES)
    qt = proj[:, :hw].T
    vt = proj[:, 2 * hw:].reshape(l // tk, tk, heads, dh).transpose(2, 0, 3, 1)
    a = jnp.stack([x[:, :heads] for x in pieces], axis=-1).reshape(l, 3 * heads)
    a = jnp.pad(a, ((0, 0), (0, LANES - 3 * heads)))
    cq = c[:, :heads].T.reshape(heads, 1, l)
    kern = functools.partial(_fox_kernel, tq=tq, tk=tk)
    return pl.pallas_call(
        kern, grid=(heads, l // tq),
        in_specs=[pl.BlockSpec((dh, tq), lambda h, i: (h, i)),
                  pl.BlockSpec((l, dh), lambda h, i: (0, heads + h)),
                  pl.BlockSpec((l, LANES), lambda h, i: (0, 0)),
                  pl.BlockSpec((None, l // tk, dh, tk), lambda h, i: (h, 0, 0, 0)),
                  pl.BlockSpec((None, 1, tq), lambda h, i: (h, 0, i))],
        out_specs=pl.BlockSpec((tq, dh), lambda h, i: (i, h)),
        out_shape=jax.ShapeDtypeStruct((l, hw), BF16),
        scratch_shapes=[pltpu.VMEM((1, tq), F32), pltpu.VMEM((1, tq), F32), pltpu.VMEM((dh, tq), F32),
                        pltpu.VMEM((2, tk, tq), F32), pltpu.VMEM((2, tk, tq), BF16),
                        pltpu.VMEM((2, 1, tq), F32)],
        compiler_params=_params(("parallel", "arbitrary")), name="fox_attn",
    )(qt, proj, a, vt, cq)


S5_GROUPS_PER_BLOCK = 16


def _s5_kernel(u_ref, perm_ref, permt_ref, b_ref, tab_ref, ptab_ref, c_ref, d_ref, o_ref,
               up_ref, xs_ref, xl_ref, xb_ref, g_ref, carry_ref, *, ns, sub):
    c = pl.program_id(1)

    @pl.when(c == 0)
    def _():
        carry_ref[...] = jnp.zeros(carry_ref.shape, F32)

    t = u_ref.shape[0]
    s = SUBLANES
    nk = t // s
    nsub = t // sub
    kps = sub // s
    zero = jnp.minimum(c, 0)

    def tab(k):
        return tab_ref[k * s:(k + 1) * s, :]

    up_ref[...] = jnp.dot(perm_ref[...], u_ref[...], preferred_element_type=F32).astype(up_ref.dtype)
    gl = 2 * LANES
    groups = [(slice(g * gl, (g + 1) * gl), slice(ns + g * gl, ns + (g + 1) * gl)) for g in range(ns // gl)]

    def project_in(b):
        xs_ref[zero + b % 2] = jnp.dot(up_ref[b * sub:(b + 1) * sub, :], b_ref[...],
                                       preferred_element_type=F32)

    x_loc = [(jnp.zeros((s, gl), F32), jnp.zeros((s, gl), F32)) for _ in groups]
    project_in(0)
    for b in range(nsub):
        if b + 1 < nsub:
            project_in(b + 1)
        slot = zero + b % 2
        for g, (re, im) in enumerate(groups):
            lam_r = ptab_ref[0:s, re]
            lam_i = ptab_ref[nk * s:(nk + 1) * s, re]
            xr, xi = x_loc[g]
            for kk in range(kps):
                rows = slice((b * kps + kk) * s, (b * kps + kk + 1) * s)
                vr = xs_ref[slot, kk * s:(kk + 1) * s, re]
                vi = xs_ref[slot, kk * s:(kk + 1) * s, im]
                xr, xi = lam_r * xr - lam_i * xi + vr, lam_r * xi + lam_i * xr + vi
                xl_ref[rows, re] = xr
                xl_ref[rows, im] = xi
            x_loc[g] = (xr, xi)

    first = lax.broadcasted_iota(jnp.int32, (s, gl), 0) == 0
    enter = []
    for g, (re, im) in enumerate(groups):
        er, ei = x_loc[g]
        for idx, d in enumerate((1, 2, 4)):
            ar, ai = tab(2 * idx)[:, re], tab(2 * idx + 1)[:, re]
            sr, si = pltpu.roll(er, d, 0), pltpu.roll(ei, d, 0)
            er, ei = er + ar * sr - ai * si, ei + ar * si + ai * sr
        cr, ci = carry_ref[0:1, re], carry_ref[0:1, im]
        pr, pi = tab(6)[:, re], tab(7)[:, re]
        er, ei = er + pr * cr - pi * ci, ei + pr * ci + pi * cr
        carry_ref[0:1, re] = er[s - 1:s, :]
        carry_ref[0:1, im] = ei[s - 1:s, :]
        enter.append((jnp.where(first, cr, pltpu.roll(er, 1, 0)), jnp.where(first, ci, pltpu.roll(ei, 1, 0))))

    for b in range(nsub):
        slot = zero + b % 2
        for g, (re, im) in enumerate(groups):
            sr, si = enter[g]
            for kk in range(kps):
                k = b * kps + kk
                rows = slice(k * s, (k + 1) * s)
                pkr = ptab_ref[k * s:(k + 1) * s, re]
                pki = ptab_ref[(nk + k) * s:(nk + k + 1) * s, re]
                xb_ref[slot, kk * s:(kk + 1) * s, re] = xl_ref[rows, re] + pkr * sr - pki * si
                xb_ref[slot, kk * s:(kk + 1) * s, im] = xl_ref[rows, im] + pkr * si + pki * sr
        y = jnp.dot(xb_ref[slot].astype(BF16), c_ref[...], preferred_element_type=F32)
        y = y + d_ref[...] * up_ref[b * sub:(b + 1) * sub, :].astype(F32)
        g_ref[b * sub:(b + 1) * sub, :] = jax.nn.gelu(y, approximate=True).astype(g_ref.dtype)
    o_ref[...] = jnp.dot(permt_ref[...], g_ref[...], preferred_element_type=F32).astype(o_ref.dtype)


def _s5_tables(lam_re, lam_im, log_dt, b_re, b_im, c_re, c_im, nk):
    g, p = lam_re.shape
    hch = b_re.shape[2]
    gb = S5_GROUPS_PER_BLOCK
    nb = g // gb
    lre = jnp.minimum(lam_re.astype(F32), S5_MAX_RE)
    lim = lam_im.astype(F32)
    lam = lax.complex(lre, lim)
    dt = jnp.exp(log_dt.astype(F32))[:, None]
    lam_bar = jnp.exp(lam * dt)
    b_bar = ((lam_bar - 1.0) / lam)[..., None] * lax.complex(b_re.astype(F32), b_im.astype(F32))
    eye = jnp.eye(gb, dtype=F32)

    def in_map(x):
        return jnp.einsum('gaph,ab->gahbp', x.reshape(nb, gb, p, hch), eye).reshape(nb, gb * hch, gb * p)

    def out_map(x):
        return jnp.einsum('gahp,ab->gapbh', x.reshape(nb, gb, hch, p), eye).reshape(nb, gb * p, gb * hch)

    b_cat = jnp.concatenate([in_map(jnp.real(b_bar)), in_map(jnp.imag(b_bar))], axis=2).astype(BF16)
    c_cat = jnp.concatenate([out_map(c_re.astype(F32)), out_map(-c_im.astype(F32))], axis=1).astype(BF16)

    def power(k):
        mag = jnp.exp(k * lre * dt)
        ang = k * lim * dt
        return ((mag * jnp.cos(ang)).reshape(nb, 1, gb * p), (mag * jnp.sin(ang)).reshape(nb, 1, gb * p))

    row = jnp.arange(SUBLANES)[None, :, None]
    parts = []
    for d in (1, 2, 4):
        re, im = power(float(nk * d))
        parts += [jnp.where(row >= d, re, 0.0), jnp.where(row >= d, im, 0.0)]
    ends = [power(float(nk * (r + 1))) for r in range(SUBLANES)]
    parts += [jnp.concatenate([e[0] for e in ends], axis=1), jnp.concatenate([e[1] for e in ends], axis=1)]
    tab = jnp.concatenate(parts, axis=1)
    steps = [power(float(k + 1)) for k in range(nk)]
    ptab = jnp.concatenate([x[0] for x in steps] + [x[1] for x in steps], axis=1)
    return b_cat, c_cat, tab, jnp.repeat(ptab, SUBLANES, axis=1)


def _s5_core(hn, lam_re, lam_im, log_dt, b_re, b_im, c_re, c_im, d):
    l, dm = hn.shape
    gb = S5_GROUPS_PER_BLOCK
    wu = gb * S5_GROUP
    ns = gb * S5_STATE
    t = _tile(l, 512, 8 * SUBLANES)
    nk = t // SUBLANES
    sub = _tile(t, 128, 16)
    b_cat, c_cat, tab, ptab = _s5_tables(lam_re, lam_im, log_dt, b_re, b_im, c_re, c_im, nk)
    nb = b_cat.shape[0]
    p = jnp.arange(t)
    perm = (p[None, :] == ((p % SUBLANES) * nk + p // SUBLANES)[:, None]).astype(BF16)
    kern = functools.partial(_s5_kernel, ns=ns, sub=sub)
    whole = lambda g, c: (0, 0)
    return pl.pallas_call(
        kern, grid=(nb, l // t),
        in_specs=[pl.BlockSpec((t, wu), lambda g, c: (c, g)),
                  pl.BlockSpec((t, t), whole),
                  pl.BlockSpec((t, t), whole),
                  pl.BlockSpec((None, wu, 2 * ns), lambda g, c: (g, 0, 0)),
                  pl.BlockSpec((None, 8 * SUBLANES, ns), lambda g, c: (g, 0, 0)),
                  pl.BlockSpec((None, 2 * nk * SUBLANES, ns), lambda g, c: (g, 0, 0)),
                  pl.BlockSpec((None, 2 * ns, wu), lambda g, c: (g, 0, 0)),
                  pl.BlockSpec((1, wu), lambda g, c: (0, g))],
        out_specs=pl.BlockSpec((t, wu), lambda g, c: (c, g)),
        out_shape=jax.ShapeDtypeStruct((l, dm), BF16),
        scratch_shapes=[pltpu.VMEM((t, wu), BF16), pltpu.VMEM((2, sub, 2 * ns), F32),
                        pltpu.VMEM((t, 2 * ns), F32), pltpu.VMEM((2, sub, 2 * ns), F32),
                        pltpu.VMEM((t, wu), BF16), pltpu.VMEM((SUBLANES, 2 * ns), F32)],
        compiler_params=_params(("parallel", "arbitrary")), name="s5_core",
    )(hn, perm, perm.T, b_cat, tab, ptab, c_cat, d.reshape(1, dm).astype(F32))


def _swa_mixer(hn, w_qkv, b_qkv, sinks, w_o, b_o):
    q_heads = sinks.shape[0]
    kv_heads = q_heads // SWA_KV_HEADS_PER_Q
    qkv = _mm(hn, w_qkv, b_qkv, BF16, tn_pref=512, name="swa_qkv")
    o = _swa_attention(qkv, sinks, q_heads, kv_heads)
    return _mm(o, w_o.astype(BF16), b_o, BF16, name="swa_out")


def _fox_mixer(hn, w_qkvf, b_f, w_o):
    heads = b_f.shape[0]
    hw = heads * FOX_HEAD_DIM
    q_scale = jnp.concatenate([jnp.full((hw,), FOX_HEAD_DIM ** -0.5 * LOG2E, F32), jnp.ones((2 * hw,), F32)])
    proj = _mm(hn, w_qkvf, None, BF16, tn_pref=512, name="fox_qkv", n=3 * hw, col_scale=q_scale)
    pad = LANES - heads
    w_f = jnp.pad(w_qkvf[:, 3 * hw:], ((0, 0), (0, pad))).astype(BF16)
    logits = _mm(hn, w_f, None, F32, name="fox_gate")
    c, hi, mid, lo = _logsig_cumsum(logits, jnp.pad(b_f.astype(F32), (0, pad)).reshape(1, LANES))
    o = _fox_attention(proj, c, (hi, mid, lo), heads)
    return _mm(o, w_o.astype(BF16), None, BF16, name="fox_out")


def _s5_mixer(hn, lam_re, lam_im, log_dt, b_re, b_im, c_re, c_im, d, w_glu, b_glu):
    g = _s5_core(hn, lam_re, lam_im, log_dt, b_re, b_im, c_re, c_im, d)
    return _glu_mm(g, w_glu.astype(BF16), b_glu, BF16)


def _conv_ffn(hn, w_in, conv_w, conv_b, w_out):
    g = _ffn_in(hn, w_in, conv_w.astype(F32), conv_b.astype(F32))
    return _mm(g, w_out.astype(BF16), None, BF16, tm_pref=512, tn_pref=512, name="ffn_out")


def _trunk(x, layers):
    b, l, d = x.shape
    assert b == 1
    h = x.reshape(l, d)
    mixers = (_swa_mixer, _fox_mixer, _s5_mixer)
    hn = _prenorm(h, layers[0][0])
    for i, (mix_pre, mix_params, mix_post, ffn_pre, ffn_params, ffn_post) in enumerate(layers):
        y = mixers[i % len(mixers)](hn, *mix_params)
        h, hn = _post_pre(y, h, mix_post, ffn_pre)
        y = _conv_ffn(hn, *ffn_params)
        nxt = layers[i + 1][0] if i + 1 < len(layers) else None
        h, hn = _post_pre(y, h, ffn_post, nxt)
    return h.reshape(b, l, d)


def kernel(x,
           l0_mix_pre_g, l0_swa_w_qkv, l0_swa_b_qkv, l0_swa_sinks, l0_swa_w_o, l0_swa_b_o, l0_mix_post_g,
           l0_ffn_pre_g, l0_ffn_w_in, l0_ffn_conv_w, l0_ffn_conv_b, l0_ffn_w_out, l0_ffn_post_g,
           l1_mix_pre_g, l1_fox_w_qkvf, l1_fox_b_f, l1_fox_w_o, l1_mix_post_g,
           l1_ffn_pre_g, l1_ffn_w_in, l1_ffn_conv_w, l1_ffn_conv_b, l1_ffn_w_out, l1_ffn_post_g,
           l2_mix_pre_g, l2_s5_lambda_re, l2_s5_lambda_im, l2_s5_log_dt, l2_s5_b_re, l2_s5_b_im,
           l2_s5_c_re, l2_s5_c_im, l2_s5_d, l2_s5_w_glu, l2_s5_b_glu, l2_mix_post_g,
           l2_ffn_pre_g, l2_ffn_w_in, l2_ffn_conv_w, l2_ffn_conv_b, l2_ffn_w_out, l2_ffn_post_g,
           l3_mix_pre_g, l3_swa_w_qkv, l3_swa_b_qkv, l3_swa_sinks, l3_swa_w_o, l3_swa_b_o, l3_mix_post_g,
           l3_ffn_pre_g, l3_ffn_w_in, l3_ffn_conv_w, l3_ffn_conv_b, l3_ffn_w_out, l3_ffn_post_g):
    layers = (
        (l0_mix_pre_g, (l0_swa_w_qkv, l0_swa_b_qkv, l0_swa_sinks, l0_swa_w_o, l0_swa_b_o), l0_mix_post_g,
         l0_ffn_pre_g, (l0_ffn_w_in, l0_ffn_conv_w, l0_ffn_conv_b, l0_ffn_w_out), l0_ffn_post_g),
        (l1_mix_pre_g, (l1_fox_w_qkvf, l1_fox_b_f, l1_fox_w_o), l1_mix_post_g,
         l1_ffn_pre_g, (l1_ffn_w_in, l1_ffn_conv_w, l1_ffn_conv_b, l1_ffn_w_out), l1_ffn_post_g),
        (l2_mix_pre_g, (l2_s5_lambda_re, l2_s5_lambda_im, l2_s5_log_dt, l2_s5_b_re, l2_s5_b_im,
                        l2_s5_c_re, l2_s5_c_im, l2_s5_d, l2_s5_w_glu, l2_s5_b_glu), l2_mix_post_g,
         l2_ffn_pre_g, (l2_ffn_w_in, l2_ffn_conv_w, l2_ffn_conv_b, l2_ffn_w_out), l2_ffn_post_g),
        (l3_mix_pre_g, (l3_swa_w_qkv, l3_swa_b_qkv, l3_swa_sinks, l3_swa_w_o, l3_swa_b_o), l3_mix_post_g,
         l3_ffn_pre_g, (l3_ffn_w_in, l3_ffn_conv_w, l3_ffn_conv_b, l3_ffn_w_out), l3_ffn_post_g),
    )
    return _trunk(x, layers)
```

```python
import functools
import math

import jax
import jax.numpy as jnp
from jax import lax
from jax.experimental import pallas as pl
from jax.experimental.pallas import tpu as pltpu

F32 = jnp.float32
BF16 = jnp.bfloat16

NORM_EPS = 1e-6
MASK_VALUE = -1e30
LOG2E = math.log2(math.e)
ATTN_BLOCK = 128
WINDOW = 128
ROPE_THETA = 500000.0
SWA_KV_HEADS_PER_Q = 8
SWA_HEAD_DIM = 64
FOX_HEAD_DIM = 128
S5_GROUP = 16
S5_STATE = 64
S5_MAX_RE = -1e-4
CONV_WIDTH = 3

LANES = 128
SUBLANES = 8
VMEM_LIMIT_BYTES = 56 * 1024 * 1024


def _params(sem, vmem=VMEM_LIMIT_BYTES):
    return pltpu.CompilerParams(dimension_semantics=sem, vmem_limit_bytes=vmem)


def _tile(dim, pref, mult):
    if dim <= pref:
        return dim
    t = (pref // mult) * mult
    while t > mult and dim % t:
        t -= mult
    assert dim % t == 0, (dim, pref, mult)
    return t


def _rms(x, g):
    ms = jnp.mean(x * x, axis=-1, keepdims=True)
    return x * lax.rsqrt(ms + NORM_EPS) * g


def _prenorm_kernel(x_ref, g_ref, o_ref):
    o_ref[...] = _rms(x_ref[...], g_ref[...]).astype(o_ref.dtype)


def _prenorm(x, g):
    m, d = x.shape
    tr = _tile(m, 256, SUBLANES)
    return pl.pallas_call(
        _prenorm_kernel,
        grid=(m // tr,),
        in_specs=[pl.BlockSpec((tr, d), lambda i: (i, 0)), pl.BlockSpec((1, d), lambda i: (0, 0))],
        out_specs=pl.BlockSpec((tr, d), lambda i: (i, 0)),
        out_shape=jax.ShapeDtypeStruct((m, d), BF16),
        compiler_params=_params(("parallel",)),
        name="prenorm",
    )(x, g.reshape(1, d))


def _post_pre_kernel(y_ref, h_ref, gp_ref, gn_ref, h_out_ref, hn_ref):
    h_new = h_ref[...] + _rms(y_ref[...].astype(F32), gp_ref[...])
    h_out_ref[...] = h_new
    hn_ref[...] = _rms(h_new, gn_ref[...]).astype(hn_ref.dtype)


def _post_kernel(y_ref, h_ref, gp_ref, h_out_ref):
    h_out_ref[...] = h_ref[...] + _rms(y_ref[...].astype(F32), gp_ref[...])


def _post_pre(y, h, g_post, g_next):
    m, d = h.shape
    tr = _tile(m, 256, SUBLANES)
    row = pl.BlockSpec((tr, d), lambda i: (i, 0))
    vec = pl.BlockSpec((1, d), lambda i: (0, 0))
    if g_next is None:
        return pl.pallas_call(
            _post_kernel, grid=(m // tr,), in_specs=[row, row, vec], out_specs=row,
            out_shape=jax.ShapeDtypeStruct((m, d), F32),
            compiler_params=_params(("parallel",)), name="post",
        )(y, h, g_post.reshape(1, d)), None
    return pl.pallas_call(
        _post_pre_kernel, grid=(m // tr,), in_specs=[row, row, vec, vec], out_specs=[row, row],
        out_shape=[jax.ShapeDtypeStruct((m, d), F32), jax.ShapeDtypeStruct((m, d), BF16)],
        compiler_params=_params(("parallel",)), name="post_pre",
    )(y, h, g_post.reshape(1, d), g_next.reshape(1, d))


def _mm_kernel(x_ref, w_ref, *rest, has_scale, has_bias):
    o_ref = rest[-1]
    acc = jnp.dot(x_ref[...], w_ref[...].astype(BF16), preferred_element_type=F32)
    if has_scale:
        acc = acc * rest[0][...]
    if has_bias:
        acc = acc + rest[-2][...]
    o_ref[...] = acc.astype(o_ref.dtype)


def _mm(x, w, b, out_dtype, tm_pref=1024, tn_pref=1024, name="mm", n=None, col_scale=None):
    m, k = x.shape
    n = w.shape[1] if n is None else n
    tm = _tile(m, tm_pref, 16)
    tn = _tile(n, tn_pref, LANES)
    vec = pl.BlockSpec((1, tn), lambda i, j: (0, j))
    in_specs = [pl.BlockSpec((tm, k), lambda i, j: (i, 0)), pl.BlockSpec((k, tn), lambda i, j: (0, j))]
    args = [x, w]
    for v in (col_scale, b):
        if v is not None:
            in_specs.append(vec)
            args.append(v.reshape(1, n).astype(F32))
    body = functools.partial(_mm_kernel, has_scale=col_scale is not None, has_bias=b is not None)
    return pl.pallas_call(
        body, grid=(m // tm, n // tn), in_specs=in_specs,
        out_specs=pl.BlockSpec((tm, tn), lambda i, j: (i, j)),
        out_shape=jax.ShapeDtypeStruct((m, n), out_dtype),
        compiler_params=_params(("parallel", "arbitrary")), name=name,
    )(*args)


def _glu_kernel(x_ref, w1_ref, w2_ref, b1_ref, b2_ref, o_ref):
    x = x_ref[...]
    z1 = jnp.dot(x, w1_ref[...], preferred_element_type=F32) + b1_ref[...]
    z2 = jnp.dot(x, w2_ref[...], preferred_element_type=F32) + b2_ref[...]
    o_ref[...] = (z1 * jax.nn.sigmoid(z2)).astype(o_ref.dtype)


def _glu_mm(x, w, b, out_dtype):
    m, k = x.shape
    n = w.shape[1] // 2
    tm = _tile(m, 1024, 16)
    tn = _tile(n, 512, LANES)
    nj = n // tn
    b2 = b.reshape(1, 2 * n).astype(F32)
    return pl.pallas_call(
        _glu_kernel, grid=(m // tm, nj),
        in_specs=[pl.BlockSpec((tm, k), lambda i, j: (i, 0)),
                  pl.BlockSpec((k, tn), lambda i, j: (0, j)),
                  pl.BlockSpec((k, tn), lambda i, j: (0, j + nj)),
                  pl.BlockSpec((1, tn), lambda i, j: (0, j)),
                  pl.BlockSpec((1, tn), lambda i, j: (0, j + nj))],
        out_specs=pl.BlockSpec((tm, tn), lambda i, j: (i, j)),
        out_shape=jax.ShapeDtypeStruct((m, n), out_dtype),
        compiler_params=_params(("parallel", "arbitrary")), name="glu_mm",
    )(x, w, w, b2, b2)


def _ffn_in_kernel(x_ref, wg_ref, wu_ref, cwg_ref, cwu_ref, cbg_ref, cbu_ref, o_ref, carry_g, carry_u,
                   zg_ref, zu_ref, *, sub):
    i = pl.program_id(0)
    j = pl.program_id(1)
    tm = x_ref.shape[0]
    tn = o_ref.shape[1]
    s = SUBLANES
    nt = sub // s

    @pl.when(i == 0)
    def _():
        carry_g[j] = jnp.zeros(carry_g.shape[1:], F32)
        carry_u[j] = jnp.zeros(carry_u.shape[1:], F32)

    sidx = lax.broadcasted_iota(jnp.int32, (nt, s, tn), 1)

    def conv(z, prev, cw_ref, cb_ref):
        z3 = z.reshape(nt, s, tn)

        def shifted(d):
            r = pltpu.roll(z3, d, 1)
            rp = jnp.concatenate([pltpu.roll(prev, d, 0)[None], r[:-1]], axis=0)
            return jnp.where(sidx < d, rp, r)

        cw = cw_ref[...]
        out = cb_ref[...] + cw[2:3, :] * z3 + cw[0:1, :] * shifted(2) + cw[1:2, :] * shifted(1)
        return out.reshape(sub, tn), z3[nt - 1]

    wg = wg_ref[...].astype(BF16)
    wu = wu_ref[...].astype(BF16)
    prev_g = carry_g[j]
    prev_u = carry_u[j]
    zero = jnp.minimum(j, 0)

    def dots(r):
        x = x_ref[r * sub:(r + 1) * sub, :]
        zg_ref[zero + r % 2] = jnp.dot(x, wg, preferred_element_type=F32)
        zu_ref[zero + r % 2] = jnp.dot(x, wu, preferred_element_type=F32)

    nsub = tm // sub
    dots(0)
    for r in range(nsub):
        if r + 1 < nsub:
            dots(r + 1)
        gate, prev_g = conv(zg_ref[zero + r % 2], prev_g, cwg_ref, cbg_ref)
        up, prev_u = conv(zu_ref[zero + r % 2], prev_u, cwu_ref, cbu_ref)
        o_ref[r * sub:(r + 1) * sub, :] = (jax.nn.gelu(gate, approximate=True) * up).astype(o_ref.dtype)
    carry_g[j] = prev_g
    carry_u[j] = prev_u


def _ffn_in(x, w_in, conv_w, conv_b):
    m, k = x.shape
    dff = w_in.shape[1] // 2
    tm = _tile(m, 2048, 16)
    tn = _tile(dff, 256, LANES)
    nj = dff // tn
    cb = conv_b.reshape(1, 2 * dff)
    sub = _tile(tm, 128, 16)
    kern = functools.partial(_ffn_in_kernel, sub=sub)
    return pl.pallas_call(
        kern, grid=(m // tm, nj),
        in_specs=[pl.BlockSpec((tm, k), lambda i, j: (i, 0)),
                  pl.BlockSpec((k, tn), lambda i, j: (0, j)),
                  pl.BlockSpec((k, tn), lambda i, j: (0, j + nj)),
                  pl.BlockSpec((CONV_WIDTH, tn), lambda i, j: (0, j)),
                  pl.BlockSpec((CONV_WIDTH, tn), lambda i, j: (0, j + nj)),
                  pl.BlockSpec((1, tn), lambda i, j: (0, j)),
                  pl.BlockSpec((1, tn), lambda i, j: (0, j + nj))],
        out_specs=pl.BlockSpec((tm, tn), lambda i, j: (i, j)),
        out_shape=jax.ShapeDtypeStruct((m, dff), BF16),
        scratch_shapes=[pltpu.VMEM((nj, SUBLANES, tn), F32), pltpu.VMEM((nj, SUBLANES, tn), F32),
                        pltpu.VMEM((2, sub, tn), F32), pltpu.VMEM((2, sub, tn), F32)],
        compiler_params=_params(("arbitrary", "arbitrary")), name="ffn_in",
    )(x, w_in, w_in, conv_w, conv_w, cb, cb)


def _swa_kernel(sinks_ref, q_ref, ko_ref, kp_ref, vo_ref, vp_ref, ro_ref, rp_ref, bias_ref, o_ref, *,
                group, scale):
    n = pl.program_id(0)
    hp = pl.program_id(1)
    blk = q_ref.shape[0]
    half = LANES // 2
    nk = 2 * blk
    gw = group * half

    def rope(x, r):
        c, s1, s2 = r[:, 0:LANES], r[:, LANES:2 * LANES], r[:, 2 * LANES:3 * LANES]
        return x * c + pltpu.roll(x, LANES - 8, 1) * s1 + pltpu.roll(x, 8, 1) * s2

    r_own = ro_ref[...]
    r_both = jnp.concatenate([rp_ref[...], r_own], axis=0)
    lane = lax.broadcasted_iota(jnp.int32, (nk, LANES), 1)
    bias = bias_ref[jnp.minimum(n, 1)]
    kv_per_step = ko_ref.shape[1] // half

    def per_head(fn):
        return jnp.concatenate([jnp.broadcast_to(fn(g), (blk, nk)) for g in range(group)], axis=1)

    for kvi in range(kv_per_step):
        kv = kvi % 2
        if kv == 0:
            lanes = slice((kvi // 2) * LANES, (kvi // 2 + 1) * LANES)
            k = rope(jnp.concatenate([kp_ref[:, lanes], ko_ref[:, lanes]], axis=0).astype(F32), r_both)
            v = jnp.concatenate([vp_ref[:, lanes], vo_ref[:, lanes]], axis=0).astype(F32)
        own = (lane < half) if kv == 0 else (lane >= half)
        k_own = jnp.where(own, k, 0.0)
        v_own = jnp.where(own, v, 0.0)
        k_sw = pltpu.roll(k_own, half, 1)
        v_sw = pltpu.roll(v_own, half, 1)
        k_pad = [x.astype(BF16) for x in ((k_own, k_sw) if kv == 0 else (k_sw, k_own))]
        v_pad = [x.astype(BF16) for x in ((v_own, v_sw) if kv == 0 else (v_sw, v_own))]
        col0 = kvi * gw
        tiles = []
        for t in range(group // 2):
            q = rope(q_ref[:, col0 + t * LANES:col0 + (t + 1) * LANES].astype(F32), r_own)
            q = (q * (scale * LOG2E)).astype(BF16)
            for hh in range(2):
                tiles.append(lax.dot_general(q, k_pad[hh], (((1,), (1,)), ((), ())),
                                             preferred_element_type=F32))
        s = jnp.concatenate(tiles, axis=1) + bias
        sinks = [sinks_ref[(kv_per_step * hp + kvi) * group + g] * LOG2E for g in range(group)]
        m = [jnp.maximum(jnp.max(s[:, g * nk:(g + 1) * nk], axis=1, keepdims=True), sinks[g])
             for g in range(group)]
        p = jnp.exp2(s - per_head(lambda g: m[g]))
        inv = per_head(lambda g: 1.0 / (jnp.sum(p[:, g * nk:(g + 1) * nk], axis=1, keepdims=True)
                                        + jnp.exp2(sinks[g] - m[g])))
        p = (p * inv).astype(BF16)
        for t in range(group // 2):
            g0 = 2 * t
            o_tile = (jnp.dot(p[:, g0 * nk:(g0 + 1) * nk], v_pad[0], preferred_element_type=F32)
                      + jnp.dot(p[:, (g0 + 1) * nk:(g0 + 2) * nk], v_pad[1], preferred_element_type=F32))
            o_ref[:, col0 + t * LANES:col0 + (t + 1) * LANES] = o_tile.astype(o_ref.dtype)


def _rope_tables(l, head_dim):
    rope_dim = head_dim // 4
    hf = rope_dim // 2
    inv_freq = jnp.exp(-math.log(ROPE_THETA) * jnp.arange(hf, dtype=F32) * (2.0 / rope_dim))
    ang = jnp.arange(l).astype(F32)[:, None] * inv_freq[None, :]
    cos, sin = jnp.cos(ang), jnp.sin(ang)
    rest = head_dim - rope_dim
    c = jnp.concatenate([cos, cos, jnp.ones((l, rest), F32)], axis=1)
    s1 = jnp.concatenate([-sin, jnp.zeros((l, head_dim - hf), F32)], axis=1)
    s2 = jnp.concatenate([jnp.zeros((l, hf), F32), sin, jnp.zeros((l, rest), F32)], axis=1)
    reps = LANES // head_dim
    return jnp.concatenate([jnp.tile(c, (1, reps)), jnp.tile(s1, (1, reps)), jnp.tile(s2, (1, reps))], axis=1)


def _swa_attention(qkv, sinks, q_heads, kv_heads):
    l = qkv.shape[0]
    dh = SWA_HEAD_DIM
    group = q_heads // kv_heads
    qw = q_heads * dh
    kvw = kv_heads * dh
    assert dh * 2 == LANES and group % 2 == 0 and kv_heads % 2 == 0
    blk = ATTN_BLOCK
    nb = l // blk
    gw = group * dh
    kps = 2
    kw = kps * dh
    k_col0 = qw // kw
    v_col0 = (qw + kvw) // kw
    rope = _rope_tables(l, dh)
    nk = 2 * blk
    qpos = jnp.arange(blk)[:, None]
    kj = jnp.arange(group * nk)[None, :] % nk
    rel = qpos + blk - kj
    band = (rel >= 0) & (rel < WINDOW)
    bias = jnp.where(jnp.stack([band & (kj >= blk), band]), 0.0, MASK_VALUE).astype(F32)
    prev = lambda n: jnp.maximum(n - 1, 0)
    kern = functools.partial(_swa_kernel, group=group, scale=dh ** -0.5)
    return pl.pallas_call(
        kern, grid=(nb, kv_heads // kps),
        in_specs=[pl.BlockSpec(memory_space=pltpu.SMEM),
                  pl.BlockSpec((blk, kps * gw), lambda n, h: (n, h)),
                  pl.BlockSpec((blk, kw), lambda n, h: (n, k_col0 + h)),
                  pl.BlockSpec((blk, kw), lambda n, h: (prev(n), k_col0 + h)),
                  pl.BlockSpec((blk, kw), lambda n, h: (n, v_col0 + h)),
                  pl.BlockSpec((blk, kw), lambda n, h: (prev(n), v_col0 + h)),
                  pl.BlockSpec((blk, 3 * LANES), lambda n, h: (n, 0)),
                  pl.BlockSpec((blk, 3 * LANES), lambda n, h: (prev(n), 0)),
                  pl.BlockSpec((2, blk, group * nk), lambda n, h: (0, 0, 0))],
        out_specs=pl.BlockSpec((blk, kps * gw), lambda n, h: (n, h)),
        out_shape=jax.ShapeDtypeStruct((l, qw), BF16),
        compiler_params=_params(("parallel", "arbitrary")), name="swa_attn",
    )(sinks.astype(F32), qkv, qkv, qkv, qkv, qkv, rope, rope, bias)


def _logsig_cumsum_kernel(x_ref, b_ref, c_ref, hi_ref, mid_ref, lo_ref, carry_ref):
    i = pl.program_id(0)

    @pl.when(i == 0)
    def _():
        carry_ref[...] = jnp.zeros(carry_ref.shape, F32)

    x = x_ref[...] + b_ref[...]
    lf = jnp.minimum(x, 0.0) - jnp.log1p(jnp.exp(-jnp.abs(x)))
    rows = lf.shape[0]
    row = lax.broadcasted_iota(jnp.int32, lf.shape, 0)
    d = 1
    while d < rows:
        lf = lf + jnp.where(row >= d, pltpu.roll(lf, d, 0), 0.0)
        d *= 2
    lf = lf + carry_ref[0:1, :]
    carry_ref[...] = jnp.broadcast_to(lf[rows - 1:rows, :], carry_ref.shape)
    c = lf * LOG2E
    c_ref[...] = c
    neg = -c
    hi = neg.astype(BF16)
    rem = neg - hi.astype(F32)
    mid = rem.astype(BF16)
    hi_ref[...] = hi
    mid_ref[...] = mid
    lo_ref[...] = (rem - mid.astype(F32)).astype(BF16)


def _logsig_cumsum(x, b):
    l, w = x.shape
    tc = _tile(l, 512, 16)
    blk = pl.BlockSpec((tc, w), lambda i: (i, 0))
    return pl.pallas_call(
        _logsig_cumsum_kernel, grid=(l // tc,),
        in_specs=[blk, pl.BlockSpec((1, w), lambda i: (0, 0))],
        out_specs=[blk, blk, blk, blk],
        out_shape=[jax.ShapeDtypeStruct((l, w), F32)] + [jax.ShapeDtypeStruct((l, w), BF16)] * 3,
        scratch_shapes=[pltpu.VMEM((SUBLANES, w), F32)],
        compiler_params=_params(("arbitrary",)), name="logsig_cumsum",
    )(x, b)


def _fox_kernel(q_ref, k_ref, a_ref, v_ref, cq_ref, o_ref, m_ref, l_ref, acc_ref, t_ref, p_ref, tmax_ref,
                alpha_ref, *, tq, tk):
    h = pl.program_id(0)
    qi = pl.program_id(1)
    col = lax.broadcasted_iota(jnp.int32, (tq, LANES), 1)
    sel = jnp.where((col >= 3 * h) & (col < 3 * h + 3), 1.0, 0.0).astype(BF16)
    qa = jnp.concatenate([q_ref[...], sel], axis=1)
    cq = cq_ref[...]
    m_ref[...] = jnp.full(m_ref.shape, MASK_VALUE, F32)
    l_ref[...] = jnp.zeros(l_ref.shape, F32)
    acc_ref[...] = jnp.zeros(acc_ref.shape, F32)
    zero = jnp.minimum(qi, 0)
    p_ref[...] = jnp.zeros(p_ref.shape, BF16)
    alpha_ref[...] = jnp.ones(alpha_ref.shape, F32)

    def logits(j, slot):
        r0 = pl.multiple_of(j * tk, tk)
        ka = jnp.concatenate([k_ref[pl.ds(r0, tk), :], a_ref[pl.ds(r0, tk), :]], axis=1)
        t = lax.dot_general(ka, qa, (((1,), (1,)), ((), ())), preferred_element_type=F32)
        t_ref[zero + slot] = t
        tmax_ref[zero + slot] = jnp.max(t, axis=0, keepdims=True)

    def pv(j, slot):
        r0 = pl.multiple_of(jnp.maximum(j, 0) * tk, tk)
        return lax.dot_general(v_ref[pl.ds(r0, tk), :], p_ref[zero + slot], (((0,), (0,)), ((), ())),
                               preferred_element_type=F32)

    def softmax(j, slot, masked, pv_old, alpha_prev):
        t = t_ref[zero + slot]
        if masked:
            kpos = j * tk + lax.broadcasted_iota(jnp.int32, t.shape, 0)
            qpos = qi * tq + lax.broadcasted_iota(jnp.int32, t.shape, 1)
            t = jnp.where(kpos <= qpos, t, MASK_VALUE)
            tmax = jnp.max(t, axis=0, keepdims=True)
        else:
            tmax = tmax_ref[zero + slot]
        m_old = m_ref[...]
        m_new = jnp.maximum(m_old, tmax + cq)
        alpha = jnp.exp2(m_old - m_new)
        p = jnp.exp2(t - (m_new - cq))
        l_ref[...] = alpha * l_ref[...] + jnp.sum(p, axis=0, keepdims=True)
        p_ref[zero + slot] = p.astype(BF16)
        acc_ref[...] = alpha * (acc_ref[...] + alpha_prev * pv_old)
        m_ref[...] = m_new
        return alpha

    def pair(a, masked):
        pv0 = pv(a - 2, 0)
        pv1 = pv(a - 1, 1)
        alpha = softmax(a, 0, masked, pv0, alpha_ref[...])
        alpha = softmax(a + 1, 1, masked, pv1, alpha)
        alpha_ref[...] = alpha
        if not masked:
            logits(a + 2, 0)
            logits(a + 3, 1)

    assert tq == 2 * tk
    logits(0, 0)
    logits(1, 1)

    def full_body(it, c):
        pair(2 * it, False)
        return c

    lax.fori_loop(0, qi, full_body, 0)
    pair(2 * qi, True)
    acc = acc_ref[...] + alpha_ref[...] * pv(2 * qi, 0) + pv(2 * qi + 1, 1)
    o_ref[...] = (acc / l_ref[...]).T.astype(o_ref.dtype)


def _fox_attention(proj, c, pieces, heads):
    l = proj.shape[0]
    dh = FOX_HEAD_DIM
    hw = heads * dh
    assert dh == LANES and 3 * heads <= LANES
    tq = _tile(l, 1024, LANES)
    tk = _tile(tq, 512, LANES)
    a = jnp.stack([x[:, :heads] for x in pieces], axis=-1).reshape(l, 3 * heads)
    a = jnp.pad(a, ((0, 0), (0, LANES - 3 * heads)))
    cq = c[:, :heads].T.reshape(heads, 1, l)
    kern = functools.partial(_fox_kernel, tq=tq, tk=tk)
    return pl.pallas_call(
        kern, grid=(heads, l // tq),
        in_specs=[pl.BlockSpec((tq, dh), lambda h, i: (i, h)),
                  pl.BlockSpec((l, dh), lambda h, i: (0, heads + h)),
                  pl.BlockSpec((l, LANES), lambda h, i: (0, 0)),
                  pl.BlockSpec((l, dh), lambda h, i: (0, 2 * heads + h)),
                  pl.BlockSpec((None, 1, tq), lambda h, i: (h, 0, i))],
        out_specs=pl.BlockSpec((tq, dh), lambda h, i: (i, h)),
        out_shape=jax.ShapeDtypeStruct((l, hw), BF16),
        scratch_shapes=[pltpu.VMEM((1, tq), F32), pltpu.VMEM((1, tq), F32), pltpu.VMEM((dh, tq), F32),
                        pltpu.VMEM((2, tk, tq), F32), pltpu.VMEM((2, tk, tq), BF16),
                        pltpu.VMEM((2, 1, tq), F32), pltpu.VMEM((1, tq), F32)],
        compiler_params=_params(("parallel", "arbitrary")), name="fox_attn",
    )(proj, proj, a, proj, cq)


S5_GROUPS_PER_BLOCK = 16


def _s5_kernel(u_ref, perm_ref, permt_ref, b_ref, tab_ref, ptab_ref, c_ref, d_ref, o_ref,
               up_ref, xs_ref, xl_ref, xb_ref, g_ref, carry_ref, *, ns, sub):
    c = pl.program_id(1)

    @pl.when(c == 0)
    def _():
        carry_ref[...] = jnp.zeros(carry_ref.shape, F32)

    t = u_ref.shape[0]
    s = SUBLANES
    nk = t // s
    nsub = t // sub
    kps = sub // s
    zero = jnp.minimum(c, 0)

    def tab(k):
        return tab_ref[k * s:(k + 1) * s, :]

    up_ref[...] = jnp.dot(perm_ref[...], u_ref[...], preferred_element_type=F32).astype(up_ref.dtype)
    gl = 2 * LANES
    groups = [(slice(g * gl, (g + 1) * gl), slice(ns + g * gl, ns + (g + 1) * gl)) for g in range(ns // gl)]

    def project_in(b):
        xs_ref[zero + b % 2] = jnp.dot(up_ref[b * sub:(b + 1) * sub, :], b_ref[...],
                                       preferred_element_type=F32)

    x_loc = [(jnp.zeros((s, gl), F32), jnp.zeros((s, gl), F32)) for _ in groups]
    project_in(0)
    for b in range(nsub):
        if b + 1 < nsub:
            project_in(b + 1)
        slot = zero + b % 2
        for g, (re, im) in enumerate(groups):
            lam_r = ptab_ref[0:s, re]
            lam_i = ptab_ref[nk * s:(nk + 1) * s, re]
            xr, xi = x_loc[g]
            for kk in range(kps):
                rows = slice((b * kps + kk) * s, (b * kps + kk + 1) * s)
                vr = xs_ref[slot, kk * s:(kk + 1) * s, re]
                vi = xs_ref[slot, kk * s:(kk + 1) * s, im]
                xr, xi = lam_r * xr - lam_i * xi + vr, lam_r * xi + lam_i * xr + vi
                xl_ref[rows, re] = xr
                xl_ref[rows, im] = xi
            x_loc[g] = (xr, xi)

    first = lax.broadcasted_iota(jnp.int32, (s, gl), 0) == 0
    enter = []
    for g, (re, im) in enumerate(groups):
        er, ei = x_loc[g]
        for idx, d in enumerate((1, 2, 4)):
            ar, ai = tab(2 * idx)[:, re], tab(2 * idx + 1)[:, re]
            sr, si = pltpu.roll(er, d, 0), pltpu.roll(ei, d, 0)
            er, ei = er + ar * sr - ai * si, ei + ar * si + ai * sr
        cr, ci = carry_ref[0:1, re], carry_ref[0:1, im]
        pr, pi = tab(6)[:, re], tab(7)[:, re]
        er, ei = er + pr * cr - pi * ci, ei + pr * ci + pi * cr
        carry_ref[0:1, re] = er[s - 1:s, :]
        carry_ref[0:1, im] = ei[s - 1:s, :]
        enter.append((jnp.where(first, cr, pltpu.roll(er, 1, 0)), jnp.where(first, ci, pltpu.roll(ei, 1, 0))))

    for b in range(nsub):
        slot = zero + b % 2
        for g, (re, im) in enumerate(groups):
            sr, si = enter[g]
            for kk in range(kps):
                k = b * kps + kk
                rows = slice(k * s, (k + 1) * s)
                pkr = ptab_ref[k * s:(k + 1) * s, re]
                pki = ptab_ref[(nk + k) * s:(nk + k + 1) * s, re]
                xb_ref[slot, kk * s:(kk + 1) * s, re] = xl_ref[rows, re] + pkr * sr - pki * si
                xb_ref[slot, kk * s:(kk + 1) * s, im] = xl_ref[rows, im] + pkr * si + pki * sr
        y = jnp.dot(xb_ref[slot].astype(BF16), c_ref[...], preferred_element_type=F32)
        y = y + d_ref[...] * up_ref[b * sub:(b + 1) * sub, :].astype(F32)
        g_ref[b * sub:(b + 1) * sub, :] = jax.nn.gelu(y, approximate=True).astype(g_ref.dtype)
    o_ref[...] = jnp.dot(permt_ref[...], g_ref[...], preferred_element_type=F32).astype(o_ref.dtype)


def _s5_tables(lam_re, lam_im, log_dt, b_re, b_im, c_re, c_im, nk):
    g, p = lam_re.shape
    hch = b_re.shape[2]
    gb = S5_GROUPS_PER_BLOCK
    nb = g // gb
    lre = jnp.minimum(lam_re.astype(F32), S5_MAX_RE)
    lim = lam_im.astype(F32)
    lam = lax.complex(lre, lim)
    dt = jnp.exp(log_dt.astype(F32))[:, None]
    lam_bar = jnp.exp(lam * dt)
    b_bar = ((lam_bar - 1.0) / lam)[..., None] * lax.complex(b_re.astype(F32), b_im.astype(F32))
    eye = jnp.eye(gb, dtype=F32)

    def in_map(x):
        return jnp.einsum('gaph,ab->gahbp', x.reshape(nb, gb, p, hch), eye).reshape(nb, gb * hch, gb * p)

    def out_map(x):
        return jnp.einsum('gahp,ab->gapbh', x.reshape(nb, gb, hch, p), eye).reshape(nb, gb * p, gb * hch)

    b_cat = jnp.concatenate([in_map(jnp.real(b_bar)), in_map(jnp.imag(b_bar))], axis=2).astype(BF16)
    c_cat = jnp.concatenate([out_map(c_re.astype(F32)), out_map(-c_im.astype(F32))], axis=1).astype(BF16)

    def power(k):
        mag = jnp.exp(k * lre * dt)
        ang = k * lim * dt
        return ((mag * jnp.cos(ang)).reshape(nb, 1, gb * p), (mag * jnp.sin(ang)).reshape(nb, 1, gb * p))

    row = jnp.arange(SUBLANES)[None, :, None]
    parts = []
    for d in (1, 2, 4):
        re, im = power(float(nk * d))
        parts += [jnp.where(row >= d, re, 0.0), jnp.where(row >= d, im, 0.0)]
    ends = [power(float(nk * (r + 1))) for r in range(SUBLANES)]
    parts += [jnp.concatenate([e[0] for e in ends], axis=1), jnp.concatenate([e[1] for e in ends], axis=1)]
    tab = jnp.concatenate(parts, axis=1)
    steps = [power(float(k + 1)) for k in range(nk)]
    ptab = jnp.concatenate([x[0] for x in steps] + [x[1] for x in steps], axis=1)
    return b_cat, c_cat, tab, jnp.repeat(ptab, SUBLANES, axis=1)


def _s5_core(hn, lam_re, lam_im, log_dt, b_re, b_im, c_re, c_im, d):
    l, dm = hn.shape
    gb = S5_GROUPS_PER_BLOCK
    wu = gb * S5_GROUP
    ns = gb * S5_STATE
    t = _tile(l, 512, 8 * SUBLANES)
    nk = t // SUBLANES
    sub = _tile(t, 128, 16)
    b_cat, c_cat, tab, ptab = _s5_tables(lam_re, lam_im, log_dt, b_re, b_im, c_re, c_im, nk)
    nb = b_cat.shape[0]
    p = jnp.arange(t)
    perm = (p[None, :] == ((p % SUBLANES) * nk + p // SUBLANES)[:, None]).astype(BF16)
    kern = functools.partial(_s5_kernel, ns=ns, sub=sub)
    whole = lambda g, c: (0, 0)
    return pl.pallas_call(
        kern, grid=(nb, l // t),
        in_specs=[pl.BlockSpec((t, wu), lambda g, c: (c, g)),
                  pl.BlockSpec((t, t), whole),
                  pl.BlockSpec((t, t), whole),
                  pl.BlockSpec((None, wu, 2 * ns), lambda g, c: (g, 0, 0)),
                  pl.BlockSpec((None, 8 * SUBLANES, ns), lambda g, c: (g, 0, 0)),
                  pl.BlockSpec((None, 2 * nk * SUBLANES, ns), lambda g, c: (g, 0, 0)),
                  pl.BlockSpec((None, 2 * ns, wu), lambda g, c: (g, 0, 0)),
                  pl.BlockSpec((1, wu), lambda g, c: (0, g))],
        out_specs=pl.BlockSpec((t, wu), lambda g, c: (c, g)),
        out_shape=jax.ShapeDtypeStruct((l, dm), BF16),
        scratch_shapes=[pltpu.VMEM((t, wu), BF16), pltpu.VMEM((2, sub, 2 * ns), F32),
                        pltpu.VMEM((t, 2 * ns), F32), pltpu.VMEM((2, sub, 2 * ns), F32),
                        pltpu.VMEM((t, wu), BF16), pltpu.VMEM((SUBLANES, 2 * ns), F32)],
        compiler_params=_params(("parallel", "arbitrary")), name="s5_core",
    )(hn, perm, perm.T, b_cat, tab, ptab, c_cat, d.reshape(1, dm).astype(F32))


def _swa_mixer(hn, w_qkv, b_qkv, sinks, w_o, b_o):
    q_heads = sinks.shape[0]
    kv_heads = q_heads // SWA_KV_HEADS_PER_Q
    qkv = _mm(hn, w_qkv, b_qkv, BF16, tn_pref=512, name="swa_qkv")
    o = _swa_attention(qkv, sinks, q_heads, kv_heads)
    return _mm(o, w_o.astype(BF16), b_o, BF16, name="swa_out")


def _fox_mixer(hn, w_qkvf, b_f, w_o):
    heads = b_f.shape[0]
    hw = heads * FOX_HEAD_DIM
    q_scale = jnp.concatenate([jnp.full((hw,), FOX_HEAD_DIM ** -0.5 * LOG2E, F32), jnp.ones((2 * hw,), F32)])
    proj = _mm(hn, w_qkvf, None, BF16, tn_pref=512, name="fox_qkv", n=3 * hw, col_scale=q_scale)
    pad = LANES - heads
    w_f = jnp.pad(w_qkvf[:, 3 * hw:], ((0, 0), (0, pad))).astype(BF16)
    logits = _mm(hn, w_f, None, F32, name="fox_gate")
    c, hi, mid, lo = _logsig_cumsum(logits, jnp.pad(b_f.astype(F32), (0, pad)).reshape(1, LANES))
    o = _fox_attention(proj, c, (hi, mid, lo), heads)
    return _mm(o, w_o.astype(BF16), None, BF16, name="fox_out")


def _s5_mixer(hn, lam_re, lam_im, log_dt, b_re, b_im, c_re, c_im, d, w_glu, b_glu):
    g = _s5_core(hn, lam_re, lam_im, log_dt, b_re, b_im, c_re, c_im, d)
    return _glu_mm(g, w_glu.astype(BF16), b_glu, BF16)


def _conv_ffn(hn, w_in, conv_w, conv_b, w_out):
    g = _ffn_in(hn, w_in, conv_w.astype(F32), conv_b.astype(F32))
    return _mm(g, w_out.astype(BF16), None, BF16, tm_pref=512, tn_pref=512, name="ffn_out")


def _trunk(x, layers):
    b, l, d = x.shape
    assert b == 1
    h = x.reshape(l, d)
    mixers = (_swa_mixer, _fox_mixer, _s5_mixer)
    hn = _prenorm(h, layers[0][0])
    for i, (mix_pre, mix_params, mix_post, ffn_pre, ffn_params, ffn_post) in enumerate(layers):
        y = mixers[i % len(mixers)](hn, *mix_params)
        h, hn = _post_pre(y, h, mix_post, ffn_pre)
        y = _conv_ffn(hn, *ffn_params)
        nxt = layers[i + 1][0] if i + 1 < len(layers) else None
        h, hn = _post_pre(y, h, ffn_post, nxt)
    return h.reshape(b, l, d)


def kernel(x,
           l0_mix_pre_g, l0_swa_w_qkv, l0_swa_b_qkv, l0_swa_sinks, l0_swa_w_o, l0_swa_b_o, l0_mix_post_g,
           l0_ffn_pre_g, l0_ffn_w_in, l0_ffn_conv_w, l0_ffn_conv_b, l0_ffn_w_out, l0_ffn_post_g,
           l1_mix_pre_g, l1_fox_w_qkvf, l1_fox_b_f, l1_fox_w_o, l1_mix_post_g,
           l1_ffn_pre_g, l1_ffn_w_in, l1_ffn_conv_w, l1_ffn_conv_b, l1_ffn_w_out, l1_ffn_post_g,
           l2_mix_pre_g, l2_s5_lambda_re, l2_s5_lambda_im, l2_s5_log_dt, l2_s5_b_re, l2_s5_b_im,
           l2_s5_c_re, l2_s5_c_im, l2_s5_d, l2_s5_w_glu, l2_s5_b_glu, l2_mix_post_g,
           l2_ffn_pre_g, l2_ffn_w_in, l2_ffn_conv_w, l2_ffn_conv_b, l2_ffn_w_out, l2_ffn_post_g,
           l3_mix_pre_g, l3_swa_w_qkv, l3_swa_b_qkv, l3_swa_sinks, l3_swa_w_o, l3_swa_b_o, l3_mix_post_g,
           l3_ffn_pre_g, l3_ffn_w_in, l3_ffn_conv_w, l3_ffn_conv_b, l3_ffn_w_out, l3_ffn_post_g):
    layers = (
        (l0_mix_pre_g, (l0_swa_w_qkv, l0_swa_b_qkv, l0_swa_sinks, l0_swa_w_o, l0_swa_b_o), l0_mix_post_g,
         l0_ffn_pre_g, (l0_ffn_w_in, l0_ffn_conv_w, l0_ffn_conv_b, l0_ffn_w_out), l0_ffn_post_g),
        (l1_mix_pre_g, (l1_fox_w_qkvf, l1_fox_b_f, l1_fox_w_o), l1_mix_post_g,
         l1_ffn_pre_g, (l1_ffn_w_in, l1_ffn_conv_w, l1_ffn_conv_b, l1_ffn_w_out), l1_ffn_post_g),
        (l2_mix_pre_g, (l2_s5_lambda_re, l2_s5_lambda_im, l2_s5_log_dt, l2_s5_b_re, l2_s5_b_im,
                        l2_s5_c_re, l2_s5_c_im, l2_s5_d, l2_s5_w_glu, l2_s5_b_glu), l2_mix_post_g,
         l2_ffn_pre_g, (l2_ffn_w_in, l2_ffn_conv_w, l2_ffn_conv_b, l2_ffn_w_out), l2_ffn_post_g),
        (l3_mix_pre_g, (l3_swa_w_qkv, l3_swa_b_qkv, l3_swa_sinks, l3_swa_w_o, l3_swa_b_o), l3_mix_post_g,
         l3_ffn_pre_g, (l3_ffn_w_in, l3_ffn_conv_w, l3_ffn_conv_b, l3_ffn_w_out), l3_ffn_post_g),
    )
    return _trunk(x, layers)
```

```python
import functools
import math

import jax
import jax.numpy as jnp
from jax import lax
from jax.experimental import pallas as pl
from jax.experimental.pallas import tpu as pltpu

F32 = jnp.float32
BF16 = jnp.bfloat16

NORM_EPS = 1e-6
MASK_VALUE = -1e30
LOG2E = math.log2(math.e)
ATTN_BLOCK = 128
WINDOW = 128
ROPE_THETA = 500000.0
SWA_KV_HEADS_PER_Q = 8
SWA_HEAD_DIM = 64
FOX_HEAD_DIM = 128
S5_GROUP = 16
S5_STATE = 64
S5_MAX_RE = -1e-4
CONV_WIDTH = 3

LANES = 128
SUBLANES = 8
VMEM_LIMIT_BYTES = 56 * 1024 * 1024


def _params(sem, vmem=VMEM_LIMIT_BYTES):
    return pltpu.CompilerParams(dimension_semantics=sem, vmem_limit_bytes=vmem)


def _tile(dim, pref, mult):
    if dim <= pref:
        return dim
    t = (pref // mult) * mult
    while t > mult and dim % t:
        t -= mult
    assert dim % t == 0, (dim, pref, mult)
    return t


def _rms(x, g):
    ms = jnp.mean(x * x, axis=-1, keepdims=True)
    return x * lax.rsqrt(ms + NORM_EPS) * g


def _prenorm_kernel(x_ref, g_ref, o_ref):
    o_ref[...] = _rms(x_ref[...], g_ref[...]).astype(o_ref.dtype)


def _prenorm(x, g):
    m, d = x.shape
    tr = _tile(m, 256, SUBLANES)
    return pl.pallas_call(
        _prenorm_kernel,
        grid=(m // tr,),
        in_specs=[pl.BlockSpec((tr, d), lambda i: (i, 0)), pl.BlockSpec((1, d), lambda i: (0, 0))],
        out_specs=pl.BlockSpec((tr, d), lambda i: (i, 0)),
        out_shape=jax.ShapeDtypeStruct((m, d), BF16),
        compiler_params=_params(("parallel",)),
        name="prenorm",
    )(x, g.reshape(1, d))


def _post_pre_kernel(y_ref, h_ref, gp_ref, gn_ref, h_out_ref, hn_ref):
    h_new = h_ref[...] + _rms(y_ref[...].astype(F32), gp_ref[...])
    h_out_ref[...] = h_new
    hn_ref[...] = _rms(h_new, gn_ref[...]).astype(hn_ref.dtype)


def _post_kernel(y_ref, h_ref, gp_ref, h_out_ref):
    h_out_ref[...] = h_ref[...] + _rms(y_ref[...].astype(F32), gp_ref[...])


def _post_pre(y, h, g_post, g_next):
    m, d = h.shape
    tr = _tile(m, 256, SUBLANES)
    row = pl.BlockSpec((tr, d), lambda i: (i, 0))
    vec = pl.BlockSpec((1, d), lambda i: (0, 0))
    if g_next is None:
        return pl.pallas_call(
            _post_kernel, grid=(m // tr,), in_specs=[row, row, vec], out_specs=row,
            out_shape=jax.ShapeDtypeStruct((m, d), F32),
            compiler_params=_params(("parallel",)), name="post",
        )(y, h, g_post.reshape(1, d)), None
    return pl.pallas_call(
        _post_pre_kernel, grid=(m // tr,), in_specs=[row, row, vec, vec], out_specs=[row, row],
        out_shape=[jax.ShapeDtypeStruct((m, d), F32), jax.ShapeDtypeStruct((m, d), BF16)],
        compiler_params=_params(("parallel",)), name="post_pre",
    )(y, h, g_post.reshape(1, d), g_next.reshape(1, d))


def _mm_kernel(x_ref, w_ref, *rest, has_scale, has_bias):
    o_ref = rest[-1]
    acc = jnp.dot(x_ref[...], w_ref[...].astype(BF16), preferred_element_type=F32)
    if has_scale:
        acc = acc * rest[0][...]
    if has_bias:
        acc = acc + rest[-2][...]
    o_ref[...] = acc.astype(o_ref.dtype)


def _mm(x, w, b, out_dtype, tm_pref=1024, tn_pref=1024, name="mm", n=None, col_scale=None):
    m, k = x.shape
    n = w.shape[1] if n is None else n
    tm = _tile(m, tm_pref, 16)
    tn = _tile(n, tn_pref, LANES)
    vec = pl.BlockSpec((1, tn), lambda i, j: (0, j))
    in_specs = [pl.BlockSpec((tm, k), lambda i, j: (i, 0)), pl.BlockSpec((k, tn), lambda i, j: (0, j))]
    args = [x, w]
    for v in (col_scale, b):
        if v is not None:
            in_specs.append(vec)
            args.append(v.reshape(1, n).astype(F32))
    body = functools.partial(_mm_kernel, has_scale=col_scale is not None, has_bias=b is not None)
    return pl.pallas_call(
        body, grid=(m // tm, n // tn), in_specs=in_specs,
        out_specs=pl.BlockSpec((tm, tn), lambda i, j: (i, j)),
        out_shape=jax.ShapeDtypeStruct((m, n), out_dtype),
        compiler_params=_params(("parallel", "arbitrary")), name=name,
    )(*args)


def _glu_kernel(x_ref, w1_ref, w2_ref, b1_ref, b2_ref, o_ref):
    x = x_ref[...]
    z1 = jnp.dot(x, w1_ref[...], preferred_element_type=F32) + b1_ref[...]
    z2 = jnp.dot(x, w2_ref[...], preferred_element_type=F32) + b2_ref[...]
    o_ref[...] = (z1 * jax.nn.sigmoid(z2)).astype(o_ref.dtype)


def _glu_mm(x, w, b, out_dtype):
    m, k = x.shape
    n = w.shape[1] // 2
    tm = _tile(m, 1024, 16)
    tn = _tile(n, 512, LANES)
    nj = n // tn
    b2 = b.reshape(1, 2 * n).astype(F32)
    return pl.pallas_call(
        _glu_kernel, grid=(m // tm, nj),
        in_specs=[pl.BlockSpec((tm, k), lambda i, j: (i, 0)),
                  pl.BlockSpec((k, tn), lambda i, j: (0, j)),
                  pl.BlockSpec((k, tn), lambda i, j: (0, j + nj)),
                  pl.BlockSpec((1, tn), lambda i, j: (0, j)),
                  pl.BlockSpec((1, tn), lambda i, j: (0, j + nj))],
        out_specs=pl.BlockSpec((tm, tn), lambda i, j: (i, j)),
        out_shape=jax.ShapeDtypeStruct((m, n), out_dtype),
        compiler_params=_params(("parallel", "arbitrary")), name="glu_mm",
    )(x, w, w, b2, b2)


def _ffn_in_kernel(x_ref, wg_ref, wu_ref, cwg_ref, cwu_ref, cbg_ref, cbu_ref, o_ref, carry_g, carry_u,
                   zg_ref, zu_ref, *, sub):
    i = pl.program_id(0)
    j = pl.program_id(1)
    tm = x_ref.shape[0]
    tn = o_ref.shape[1]
    s = SUBLANES
    nt = sub // s

    @pl.when(i == 0)
    def _():
        carry_g[j] = jnp.zeros(carry_g.shape[1:], F32)
        carry_u[j] = jnp.zeros(carry_u.shape[1:], F32)

    sidx = lax.broadcasted_iota(jnp.int32, (nt, s, tn), 1)

    def conv(z, prev, cw_ref, cb_ref):
        z3 = z.reshape(nt, s, tn)

        def shifted(d):
            r = pltpu.roll(z3, d, 1)
            rp = jnp.concatenate([pltpu.roll(prev, d, 0)[None], r[:-1]], axis=0)
            return jnp.where(sidx < d, rp, r)

        cw = cw_ref[...]
        out = cb_ref[...] + cw[2:3, :] * z3 + cw[0:1, :] * shifted(2) + cw[1:2, :] * shifted(1)
        return out.reshape(sub, tn), z3[nt - 1]

    wg = wg_ref[...].astype(BF16)
    wu = wu_ref[...].astype(BF16)
    prev_g = carry_g[j]
    prev_u = carry_u[j]
    zero = jnp.minimum(j, 0)

    def dots(r):
        x = x_ref[r * sub:(r + 1) * sub, :]
        zg_ref[zero + r % 2] = jnp.dot(x, wg, preferred_element_type=F32)
        zu_ref[zero + r % 2] = jnp.dot(x, wu, preferred_element_type=F32)

    nsub = tm // sub
    dots(0)
    for r in range(nsub):
        if r + 1 < nsub:
            dots(r + 1)
        gate, prev_g = conv(zg_ref[zero + r % 2], prev_g, cwg_ref, cbg_ref)
        up, prev_u = conv(zu_ref[zero + r % 2], prev_u, cwu_ref, cbu_ref)
        o_ref[r * sub:(r + 1) * sub, :] = (jax.nn.gelu(gate, approximate=True) * up).astype(o_ref.dtype)
    carry_g[j] = prev_g
    carry_u[j] = prev_u


def _ffn_in(x, w_in, conv_w, conv_b):
    m, k = x.shape
    dff = w_in.shape[1] // 2
    tm = _tile(m, 2048, 16)
    tn = _tile(dff, 256, LANES)
    nj = dff // tn
    cb = conv_b.reshape(1, 2 * dff)
    sub = _tile(tm, 128, 16)
    kern = functools.partial(_ffn_in_kernel, sub=sub)
    return pl.pallas_call(
        kern, grid=(m // tm, nj),
        in_specs=[pl.BlockSpec((tm, k), lambda i, j: (i, 0)),
                  pl.BlockSpec((k, tn), lambda i, j: (0, j)),
                  pl.BlockSpec((k, tn), lambda i, j: (0, j + nj)),
                  pl.BlockSpec((CONV_WIDTH, tn), lambda i, j: (0, j)),
                  pl.BlockSpec((CONV_WIDTH, tn), lambda i, j: (0, j + nj)),
                  pl.BlockSpec((1, tn), lambda i, j: (0, j)),
                  pl.BlockSpec((1, tn), lambda i, j: (0, j + nj))],
        out_specs=pl.BlockSpec((tm, tn), lambda i, j: (i, j)),
        out_shape=jax.ShapeDtypeStruct((m, dff), BF16),
        scratch_shapes=[pltpu.VMEM((nj, SUBLANES, tn), F32), pltpu.VMEM((nj, SUBLANES, tn), F32),
                        pltpu.VMEM((2, sub, tn), F32), pltpu.VMEM((2, sub, tn), F32)],
        compiler_params=_params(("arbitrary", "arbitrary")), name="ffn_in",
    )(x, w_in, w_in, conv_w, conv_w, cb, cb)


def _swa_kernel(sinks_ref, q_ref, ko_ref, kp_ref, vo_ref, vp_ref, ro_ref, rp_ref, bias_ref, o_ref, *,
                group, scale):
    n = pl.program_id(0)
    hp = pl.program_id(1)
    blk = q_ref.shape[0]
    half = LANES // 2
    nk = 2 * blk
    gw = group * half

    def rope(x, r):
        c, s1, s2 = r[:, 0:LANES], r[:, LANES:2 * LANES], r[:, 2 * LANES:3 * LANES]
        return x * c + pltpu.roll(x, LANES - 8, 1) * s1 + pltpu.roll(x, 8, 1) * s2

    r_own = ro_ref[...]
    r_both = jnp.concatenate([rp_ref[...], r_own], axis=0)
    lane = lax.broadcasted_iota(jnp.int32, (nk, LANES), 1)
    bias = bias_ref[jnp.minimum(n, 1)]
    kv_per_step = ko_ref.shape[1] // half

    def per_head(fn):
        return jnp.concatenate([jnp.broadcast_to(fn(g), (blk, nk)) for g in range(group)], axis=1)

    for kvi in range(kv_per_step):
        kv = kvi % 2
        if kv == 0:
            lanes = slice((kvi // 2) * LANES, (kvi // 2 + 1) * LANES)
            k = rope(jnp.concatenate([kp_ref[:, lanes], ko_ref[:, lanes]], axis=0).astype(F32), r_both)
            v = jnp.concatenate([vp_ref[:, lanes], vo_ref[:, lanes]], axis=0).astype(F32)
        own = (lane < half) if kv == 0 else (lane >= half)
        k_own = jnp.where(own, k, 0.0)
        v_own = jnp.where(own, v, 0.0)
        k_sw = pltpu.roll(k_own, half, 1)
        v_sw = pltpu.roll(v_own, half, 1)
        k_pad = [x.astype(BF16) for x in ((k_own, k_sw) if kv == 0 else (k_sw, k_own))]
        v_pad = [x.astype(BF16) for x in ((v_own, v_sw) if kv == 0 else (v_sw, v_own))]
        col0 = kvi * gw
        tiles = []
        for t in range(group // 2):
            q = rope(q_ref[:, col0 + t * LANES:col0 + (t + 1) * LANES].astype(F32), r_own)
            q = (q * (scale * LOG2E)).astype(BF16)
            for hh in range(2):
                tiles.append(lax.dot_general(q, k_pad[hh], (((1,), (1,)), ((), ())),
                                             preferred_element_type=F32))
        s = jnp.concatenate(tiles, axis=1) + bias
        sinks = [sinks_ref[(kv_per_step * hp + kvi) * group + g] * LOG2E for g in range(group)]
        m = [jnp.maximum(jnp.max(s[:, g * nk:(g + 1) * nk], axis=1, keepdims=True), sinks[g])
             for g in range(group)]
        p = jnp.exp2(s - per_head(lambda g: m[g]))
        inv = per_head(lambda g: 1.0 / (jnp.sum(p[:, g * nk:(g + 1) * nk], axis=1, keepdims=True)
                                        + jnp.exp2(sinks[g] - m[g])))
        p = (p * inv).astype(BF16)
        for t in range(group // 2):
            g0 = 2 * t
            o_tile = (jnp.dot(p[:, g0 * nk:(g0 + 1) * nk], v_pad[0], preferred_element_type=F32)
                      + jnp.dot(p[:, (g0 + 1) * nk:(g0 + 2) * nk], v_pad[1], preferred_element_type=F32))
            o_ref[:, col0 + t * LANES:col0 + (t + 1) * LANES] = o_tile.astype(o_ref.dtype)


def _rope_tables(l, head_dim):
    rope_dim = head_dim // 4
    hf = rope_dim // 2
    inv_freq = jnp.exp(-math.log(ROPE_THETA) * jnp.arange(hf, dtype=F32) * (2.0 / rope_dim))
    ang = jnp.arange(l).astype(F32)[:, None] * inv_freq[None, :]
    cos, sin = jnp.cos(ang), jnp.sin(ang)
    rest = head_dim - rope_dim
    c = jnp.concatenate([cos, cos, jnp.ones((l, rest), F32)], axis=1)
    s1 = jnp.concatenate([-sin, jnp.zeros((l, head_dim - hf), F32)], axis=1)
    s2 = jnp.concatenate([jnp.zeros((l, hf), F32), sin, jnp.zeros((l, rest), F32)], axis=1)
    reps = LANES // head_dim
    return jnp.concatenate([jnp.tile(c, (1, reps)), jnp.tile(s1, (1, reps)), jnp.tile(s2, (1, reps))], axis=1)


def _swa_attention(qkv, sinks, q_heads, kv_heads):
    l = qkv.shape[0]
    dh = SWA_HEAD_DIM
    group = q_heads // kv_heads
    qw = q_heads * dh
    kvw = kv_heads * dh
    assert dh * 2 == LANES and group % 2 == 0 and kv_heads % 2 == 0
    blk = ATTN_BLOCK
    nb = l // blk
    gw = group * dh
    kps = 2
    kw = kps * dh
    k_col0 = qw // kw
    v_col0 = (qw + kvw) // kw
    rope = _rope_tables(l, dh)
    nk = 2 * blk
    qpos = jnp.arange(blk)[:, None]
    kj = jnp.arange(group * nk)[None, :] % nk
    rel = qpos + blk - kj
    band = (rel >= 0) & (rel < WINDOW)
    bias = jnp.where(jnp.stack([band & (kj >= blk), band]), 0.0, MASK_VALUE).astype(F32)
    prev = lambda n: jnp.maximum(n - 1, 0)
    kern = functools.partial(_swa_kernel, group=group, scale=dh ** -0.5)
    return pl.pallas_call(
        kern, grid=(nb, kv_heads // kps),
        in_specs=[pl.BlockSpec(memory_space=pltpu.SMEM),
                  pl.BlockSpec((blk, kps * gw), lambda n, h: (n, h)),
                  pl.BlockSpec((blk, kw), lambda n, h: (n, k_col0 + h)),
                  pl.BlockSpec((blk, kw), lambda n, h: (prev(n), k_col0 + h)),
                  pl.BlockSpec((blk, kw), lambda n, h: (n, v_col0 + h)),
                  pl.BlockSpec((blk, kw), lambda n, h: (prev(n), v_col0 + h)),
                  pl.BlockSpec((blk, 3 * LANES), lambda n, h: (n, 0)),
                  pl.BlockSpec((blk, 3 * LANES), lambda n, h: (prev(n), 0)),
                  pl.BlockSpec((2, blk, group * nk), lambda n, h: (0, 0, 0))],
        out_specs=pl.BlockSpec((blk, kps * gw), lambda n, h: (n, h)),
        out_shape=jax.ShapeDtypeStruct((l, qw), BF16),
        compiler_params=_params(("parallel", "arbitrary")), name="swa_attn",
    )(sinks.astype(F32), qkv, qkv, qkv, qkv, qkv, rope, rope, bias)


def _logsig_cumsum_kernel(x_ref, b_ref, c_ref, hi_ref, mid_ref, lo_ref, carry_ref):
    i = pl.program_id(0)

    @pl.when(i == 0)
    def _():
        carry_ref[...] = jnp.zeros(carry_ref.shape, F32)

    x = x_ref[...] + b_ref[...]
    lf = jnp.minimum(x, 0.0) - jnp.log1p(jnp.exp(-jnp.abs(x)))
    rows = lf.shape[0]
    row = lax.broadcasted_iota(jnp.int32, lf.shape, 0)
    d = 1
    while d < rows:
        lf = lf + jnp.where(row >= d, pltpu.roll(lf, d, 0), 0.0)
        d *= 2
    lf = lf + carry_ref[0:1, :]
    carry_ref[...] = jnp.broadcast_to(lf[rows - 1:rows, :], carry_ref.shape)
    c = lf * LOG2E
    c_ref[...] = c
    neg = -c
    hi = neg.astype(BF16)
    rem = neg - hi.astype(F32)
    mid = rem.astype(BF16)
    hi_ref[...] = hi
    mid_ref[...] = mid
    lo_ref[...] = (rem - mid.astype(F32)).astype(BF16)


def _logsig_cumsum(x, b):
    l, w = x.shape
    tc = _tile(l, 512, 16)
    blk = pl.BlockSpec((tc, w), lambda i: (i, 0))
    return pl.pallas_call(
        _logsig_cumsum_kernel, grid=(l // tc,),
        in_specs=[blk, pl.BlockSpec((1, w), lambda i: (0, 0))],
        out_specs=[blk, blk, blk, blk],
        out_shape=[jax.ShapeDtypeStruct((l, w), F32)] + [jax.ShapeDtypeStruct((l, w), BF16)] * 3,
        scratch_shapes=[pltpu.VMEM((SUBLANES, w), F32)],
        compiler_params=_params(("arbitrary",)), name="logsig_cumsum",
    )(x, b)


def _fox_kernel(q_ref, k_ref, a_ref, v_ref, cq_ref, mask_ref, o_ref, m_ref, l_ref, acc_ref, t_ref, p_ref,
                tmax_ref, alpha_ref, *, tq, tk):
    h = pl.program_id(0)
    qi = pl.program_id(1)
    col = lax.broadcasted_iota(jnp.int32, (tq, LANES), 1)
    sel = jnp.where((col >= 3 * h) & (col < 3 * h + 3), 1.0, 0.0).astype(BF16)
    qa = jnp.concatenate([q_ref[...], sel], axis=1)
    cq = cq_ref[...]
    m_ref[...] = jnp.full(m_ref.shape, MASK_VALUE, F32)
    l_ref[...] = jnp.zeros(l_ref.shape, F32)
    acc_ref[...] = jnp.zeros(acc_ref.shape, F32)
    zero = jnp.minimum(qi, 0)
    p_ref[...] = jnp.zeros(p_ref.shape, BF16)
    alpha_ref[...] = jnp.ones(alpha_ref.shape, F32)

    def logits(j, slot):
        r0 = pl.multiple_of(j * tk, tk)
        ka = jnp.concatenate([k_ref[pl.ds(r0, tk), :], a_ref[pl.ds(r0, tk), :]], axis=1)
        t = lax.dot_general(ka, qa, (((1,), (1,)), ((), ())), preferred_element_type=F32)
        t_ref[zero + slot] = t
        tmax_ref[zero + slot] = jnp.max(t, axis=0, keepdims=True)

    def pv(j, slot):
        r0 = pl.multiple_of(jnp.maximum(j, 0) * tk, tk)
        return lax.dot_general(v_ref[pl.ds(r0, tk), :], p_ref[zero + slot], (((0,), (0,)), ((), ())),
                               preferred_element_type=F32)

    def softmax(j, slot, masked, pv_old, alpha_prev):
        t = t_ref[zero + slot]
        if masked:
            t = t + mask_ref[slot]
            tmax = jnp.max(t, axis=0, keepdims=True)
        else:
            tmax = tmax_ref[zero + slot]
        m_old = m_ref[...]
        m_new = jnp.maximum(m_old, tmax + cq)
        alpha = jnp.exp2(m_old - m_new)
        p = jnp.exp2(t - (m_new - cq))
        l_ref[...] = alpha * l_ref[...] + jnp.sum(p, axis=0, keepdims=True)
        p_ref[zero + slot] = p.astype(BF16)
        acc_ref[...] = alpha * (acc_ref[...] + alpha_prev * pv_old)
        m_ref[...] = m_new
        return alpha

    def pair(a, masked):
        pv0 = pv(a - 2, 0)
        pv1 = pv(a - 1, 1)
        alpha = softmax(a, 0, masked, pv0, alpha_ref[...])
        alpha = softmax(a + 1, 1, masked, pv1, alpha)
        alpha_ref[...] = alpha
        if not masked:
            logits(a + 2, 0)
            logits(a + 3, 1)

    assert tq == 2 * tk
    logits(0, 0)
    logits(1, 1)

    def full_body(it, c):
        pair(2 * it, False)
        return c

    lax.fori_loop(0, qi, full_body, 0)
    pair(2 * qi, True)
    acc = acc_ref[...] + alpha_ref[...] * pv(2 * qi, 0) + pv(2 * qi + 1, 1)
    o_ref[...] = (acc / l_ref[...]).T.astype(o_ref.dtype)


def _fox_attention(proj, c, pieces, heads):
    l = proj.shape[0]
    dh = FOX_HEAD_DIM
    hw = heads * dh
    assert dh == LANES and 3 * heads <= LANES
    tq = _tile(l, 1024, LANES)
    tk = _tile(tq, 512, LANES)
    a = jnp.stack([x[:, :heads] for x in pieces], axis=-1).reshape(l, 3 * heads)
    a = jnp.pad(a, ((0, 0), (0, LANES - 3 * heads)))
    cq = c[:, :heads].T.reshape(heads, 1, l)
    kr = jnp.arange(tk)[:, None]
    qc = jnp.arange(tq)[None, :]
    mask = jnp.where(jnp.stack([kr <= qc, tk + kr <= qc]), 0.0, MASK_VALUE).astype(F32)
    kern = functools.partial(_fox_kernel, tq=tq, tk=tk)
    return pl.pallas_call(
        kern, grid=(heads, l // tq),
        in_specs=[pl.BlockSpec((tq, dh), lambda h, i: (i, h)),
                  pl.BlockSpec((l, dh), lambda h, i: (0, heads + h)),
                  pl.BlockSpec((l, LANES), lambda h, i: (0, 0)),
                  pl.BlockSpec((l, dh), lambda h, i: (0, 2 * heads + h)),
                  pl.BlockSpec((None, 1, tq), lambda h, i: (h, 0, i)),
                  pl.BlockSpec((2, tk, tq), lambda h, i: (0, 0, 0))],
        out_specs=pl.BlockSpec((tq, dh), lambda h, i: (i, h)),
        out_shape=jax.ShapeDtypeStruct((l, hw), BF16),
        scratch_shapes=[pltpu.VMEM((1, tq), F32), pltpu.VMEM((1, tq), F32), pltpu.VMEM((dh, tq), F32),
                        pltpu.VMEM((2, tk, tq), F32), pltpu.VMEM((2, tk, tq), BF16),
                        pltpu.VMEM((2, 1, tq), F32), pltpu.VMEM((1, tq), F32)],
        compiler_params=_params(("parallel", "arbitrary")), name="fox_attn",
    )(proj, proj, a, proj, cq, mask)


S5_GROUPS_PER_BLOCK = 16


def _s5_kernel(u_ref, perm_ref, permt_ref, b_ref, tab_ref, ptab_ref, c_ref, d_ref, o_ref,
               up_ref, xs_ref, xl_ref, xb_ref, g_ref, carry_ref, *, ns, sub):
    c = pl.program_id(1)

    @pl.when(c == 0)
    def _():
        carry_ref[...] = jnp.zeros(carry_ref.shape, F32)

    t = u_ref.shape[0]
    s = SUBLANES
    nk = t // s
    nsub = t // sub
    kps = sub // s
    zero = jnp.minimum(c, 0)

    def tab(k):
        return tab_ref[k * s:(k + 1) * s, :]

    up_ref[...] = jnp.dot(perm_ref[...], u_ref[...], preferred_element_type=F32).astype(up_ref.dtype)
    gl = 2 * LANES
    groups = [(slice(g * gl, (g + 1) * gl), slice(ns + g * gl, ns + (g + 1) * gl)) for g in range(ns // gl)]

    def project_in(b):
        xs_ref[zero + b % 2] = jnp.dot(up_ref[b * sub:(b + 1) * sub, :], b_ref[...],
                                       preferred_element_type=F32)

    x_loc = [(jnp.zeros((s, gl), F32), jnp.zeros((s, gl), F32)) for _ in groups]
    project_in(0)
    for b in range(nsub):
        if b + 1 < nsub:
            project_in(b + 1)
        slot = zero + b % 2
        for g, (re, im) in enumerate(groups):
            lam_r = ptab_ref[0:s, re]
            lam_i = ptab_ref[nk * s:(nk + 1) * s, re]
            xr, xi = x_loc[g]
            for kk in range(kps):
                rows = slice((b * kps + kk) * s, (b * kps + kk + 1) * s)
                vr = xs_ref[slot, kk * s:(kk + 1) * s, re]
                vi = xs_ref[slot, kk * s:(kk + 1) * s, im]
                xr, xi = lam_r * xr - lam_i * xi + vr, lam_r * xi + lam_i * xr + vi
                xl_ref[rows, re] = xr
                xl_ref[rows, im] = xi
            x_loc[g] = (xr, xi)

    first = lax.broadcasted_iota(jnp.int32, (s, gl), 0) == 0
    enter = []
    for g, (re, im) in enumerate(groups):
        er, ei = x_loc[g]
        for idx, d in enumerate((1, 2, 4)):
            ar, ai = tab(2 * idx)[:, re], tab(2 * idx + 1)[:, re]
            sr, si = pltpu.roll(er, d, 0), pltpu.roll(ei, d, 0)
            er, ei = er + ar * sr - ai * si, ei + ar * si + ai * sr
        cr, ci = carry_ref[0:1, re], carry_ref[0:1, im]
        pr, pi = tab(6)[:, re], tab(7)[:, re]
        er, ei = er + pr * cr - pi * ci, ei + pr * ci + pi * cr
        carry_ref[0:1, re] = er[s - 1:s, :]
        carry_ref[0:1, im] = ei[s - 1:s, :]
        enter.append((jnp.where(first, cr, pltpu.roll(er, 1, 0)), jnp.where(first, ci, pltpu.roll(ei, 1, 0))))

    for b in range(nsub):
        slot = zero + b % 2
        for g, (re, im) in enumerate(groups):
            sr, si = enter[g]
            for kk in range(kps):
                k = b * kps + kk
                rows = slice(k * s, (k + 1) * s)
                pkr = ptab_ref[k * s:(k + 1) * s, re]
                pki = ptab_ref[(nk + k) * s:(nk + k + 1) * s, re]
                xb_ref[slot, kk * s:(kk + 1) * s, re] = xl_ref[rows, re] + pkr * sr - pki * si
                xb_ref[slot, kk * s:(kk + 1) * s, im] = xl_ref[rows, im] + pkr * si + pki * sr
        y = jnp.dot(xb_ref[slot].astype(BF16), c_ref[...], preferred_element_type=F32)
        y = y + d_ref[...] * up_ref[b * sub:(b + 1) * sub, :].astype(F32)
        g_ref[b * sub:(b + 1) * sub, :] = jax.nn.gelu(y, approximate=True).astype(g_ref.dtype)
    o_ref[...] = jnp.dot(permt_ref[...], g_ref[...], preferred_element_type=F32).astype(o_ref.dtype)


def _s5_tables(lam_re, lam_im, log_dt, b_re, b_im, c_re, c_im, nk):
    g, p = lam_re.shape
    hch = b_re.shape[2]
    gb = S5_GROUPS_PER_BLOCK
    nb = g // gb
    lre = jnp.minimum(lam_re.astype(F32), S5_MAX_RE)
    lim = lam_im.astype(F32)
    lam = lax.complex(lre, lim)
    dt = jnp.exp(log_dt.astype(F32))[:, None]
    lam_bar = jnp.exp(lam * dt)
    b_bar = ((lam_bar - 1.0) / lam)[..., None] * lax.complex(b_re.astype(F32), b_im.astype(F32))
    eye = jnp.eye(gb, dtype=F32)

    def in_map(x):
        return jnp.einsum('gaph,ab->gahbp', x.reshape(nb, gb, p, hch), eye).reshape(nb, gb * hch, gb * p)

    def out_map(x):
        return jnp.einsum('gahp,ab->gapbh', x.reshape(nb, gb, hch, p), eye).reshape(nb, gb * p, gb * hch)

    b_cat = jnp.concatenate([in_map(jnp.real(b_bar)), in_map(jnp.imag(b_bar))], axis=2).astype(BF16)
    c_cat = jnp.concatenate([out_map(c_re.astype(F32)), out_map(-c_im.astype(F32))], axis=1).astype(BF16)

    log_mag = (lre * dt).reshape(nb, 1, gb * p)
    phase = (lim * dt).reshape(nb, 1, gb * p)

    def powers(ks):
        k = jnp.asarray(list(ks), F32)[None, :, None]
        mag = jnp.exp(k * log_mag)
        return mag * jnp.cos(k * phase), mag * jnp.sin(k * phase)

    row = jnp.arange(SUBLANES)[None, :, None]
    parts = []
    for d in (1, 2, 4):
        re, im = powers([nk * d])
        parts += [jnp.where(row >= d, re, 0.0), jnp.where(row >= d, im, 0.0)]
    parts += list(powers([nk * (r + 1) for r in range(SUBLANES)]))
    tab = jnp.concatenate(parts, axis=1)
    ptab = jnp.concatenate(powers(range(1, nk + 1)), axis=1)
    return b_cat, c_cat, tab, jnp.repeat(ptab, SUBLANES, axis=1)


def _s5_core(hn, lam_re, lam_im, log_dt, b_re, b_im, c_re, c_im, d):
    l, dm = hn.shape
    gb = S5_GROUPS_PER_BLOCK
    wu = gb * S5_GROUP
    ns = gb * S5_STATE
    t = _tile(l, 512, 8 * SUBLANES)
    nk = t // SUBLANES
    sub = _tile(t, 128, 16)
    b_cat, c_cat, tab, ptab = _s5_tables(lam_re, lam_im, log_dt, b_re, b_im, c_re, c_im, nk)
    nb = b_cat.shape[0]
    p = jnp.arange(t)
    perm = (p[None, :] == ((p % SUBLANES) * nk + p // SUBLANES)[:, None]).astype(BF16)
    kern = functools.partial(_s5_kernel, ns=ns, sub=sub)
    whole = lambda g, c: (0, 0)
    return pl.pallas_call(
        kern, grid=(nb, l // t),
        in_specs=[pl.BlockSpec((t, wu), lambda g, c: (c, g)),
                  pl.BlockSpec((t, t), whole),
                  pl.BlockSpec((t, t), whole),
                  pl.BlockSpec((None, wu, 2 * ns), lambda g, c: (g, 0, 0)),
                  pl.BlockSpec((None, 8 * SUBLANES, ns), lambda g, c: (g, 0, 0)),
                  pl.BlockSpec((None, 2 * nk * SUBLANES, ns), lambda g, c: (g, 0, 0)),
                  pl.BlockSpec((None, 2 * ns, wu), lambda g, c: (g, 0, 0)),
                  pl.BlockSpec((1, wu), lambda g, c: (0, g))],
        out_specs=pl.BlockSpec((t, wu), lambda g, c: (c, g)),
        out_shape=jax.ShapeDtypeStruct((l, dm), BF16),
        scratch_shapes=[pltpu.VMEM((t, wu), BF16), pltpu.VMEM((2, sub, 2 * ns), F32),
                        pltpu.VMEM((t, 2 * ns), F32), pltpu.VMEM((2, sub, 2 * ns), F32),
                        pltpu.VMEM((t, wu), BF16), pltpu.VMEM((SUBLANES, 2 * ns), F32)],
        compiler_params=_params(("parallel", "arbitrary")), name="s5_core",
    )(hn, perm, perm.T, b_cat, tab, ptab, c_cat, d.reshape(1, dm).astype(F32))


def _swa_mixer(hn, w_qkv, b_qkv, sinks, w_o, b_o):
    q_heads = sinks.shape[0]
    kv_heads = q_heads // SWA_KV_HEADS_PER_Q
    qkv = _mm(hn, w_qkv, b_qkv, BF16, tn_pref=512, name="swa_qkv")
    o = _swa_attention(qkv, sinks, q_heads, kv_heads)
    return _mm(o, w_o.astype(BF16), b_o, BF16, name="swa_out")


def _fox_mixer(hn, w_qkvf, b_f, w_o):
    heads = b_f.shape[0]
    hw = heads * FOX_HEAD_DIM
    q_scale = jnp.concatenate([jnp.full((hw,), FOX_HEAD_DIM ** -0.5 * LOG2E, F32), jnp.ones((2 * hw,), F32)])
    proj = _mm(hn, w_qkvf, None, BF16, tn_pref=512, name="fox_qkv", n=3 * hw, col_scale=q_scale)
    pad = LANES - heads
    w_f = jnp.pad(w_qkvf[:, 3 * hw:], ((0, 0), (0, pad))).astype(BF16)
    logits = _mm(hn, w_f, None, F32, name="fox_gate")
    c, hi, mid, lo = _logsig_cumsum(logits, jnp.pad(b_f.astype(F32), (0, pad)).reshape(1, LANES))
    o = _fox_attention(proj, c, (hi, mid, lo), heads)
    return _mm(o, w_o.astype(BF16), None, BF16, name="fox_out")


def _s5_mixer(hn, lam_re, lam_im, log_dt, b_re, b_im, c_re, c_im, d, w_glu, b_glu):
    g = _s5_core(hn, lam_re, lam_im, log_dt, b_re, b_im, c_re, c_im, d)
    return _glu_mm(g, w_glu.astype(BF16), b_glu, BF16)


def _conv_ffn(hn, w_in, conv_w, conv_b, w_out):
    g = _ffn_in(hn, w_in, conv_w.astype(F32), conv_b.astype(F32))
    return _mm(g, w_out.astype(BF16), None, BF16, tm_pref=512, tn_pref=512, name="ffn_out")


def _trunk(x, layers):
    b, l, d = x.shape
    assert b == 1
    h = x.reshape(l, d)
    mixers = (_swa_mixer, _fox_mixer, _s5_mixer)
    hn = _prenorm(h, layers[0][0])
    for i, (mix_pre, mix_params, mix_post, ffn_pre, ffn_params, ffn_post) in enumerate(layers):
        y = mixers[i % len(mixers)](hn, *mix_params)
        h, hn = _post_pre(y, h, mix_post, ffn_pre)
        y = _conv_ffn(hn, *ffn_params)
        nxt = layers[i + 1][0] if i + 1 < len(layers) else None
        h, hn = _post_pre(y, h, ffn_post, nxt)
    return h.reshape(b, l, d)


def kernel(x,
           l0_mix_pre_g, l0_swa_w_qkv, l0_swa_b_qkv, l0_swa_sinks, l0_swa_w_o, l0_swa_b_o, l0_mix_post_g,
           l0_ffn_pre_g, l0_ffn_w_in, l0_ffn_conv_w, l0_ffn_conv_b, l0_ffn_w_out, l0_ffn_post_g,
           l1_mix_pre_g, l1_fox_w_qkvf, l1_fox_b_f, l1_fox_w_o, l1_mix_post_g,
           l1_ffn_pre_g, l1_ffn_w_in, l1_ffn_conv_w, l1_ffn_conv_b, l1_ffn_w_out, l1_ffn_post_g,
           l2_mix_pre_g, l2_s5_lambda_re, l2_s5_lambda_im, l2_s5_log_dt, l2_s5_b_re, l2_s5_b_im,
           l2_s5_c_re, l2_s5_c_im, l2_s5_d, l2_s5_w_glu, l2_s5_b_glu, l2_mix_post_g,
           l2_ffn_pre_g, l2_ffn_w_in, l2_ffn_conv_w, l2_ffn_conv_b, l2_ffn_w_out, l2_ffn_post_g,
           l3_mix_pre_g, l3_swa_w_qkv, l3_swa_b_qkv, l3_swa_sinks, l3_swa_w_o, l3_swa_b_o, l3_mix_post_g,
           l3_ffn_pre_g, l3_ffn_w_in, l3_ffn_conv_w, l3_ffn_conv_b, l3_ffn_w_out, l3_ffn_post_g):
    layers = (
        (l0_mix_pre_g, (l0_swa_w_qkv, l0_swa_b_qkv, l0_swa_sinks, l0_swa_w_o, l0_swa_b_o), l0_mix_post_g,
         l0_ffn_pre_g, (l0_ffn_w_in, l0_ffn_conv_w, l0_ffn_conv_b, l0_ffn_w_out), l0_ffn_post_g),
        (l1_mix_pre_g, (l1_fox_w_qkvf, l1_fox_b_f, l1_fox_w_o), l1_mix_post_g,
         l1_ffn_pre_g, (l1_ffn_w_in, l1_ffn_conv_w, l1_ffn_conv_b, l1_ffn_w_out), l1_ffn_post_g),
        (l2_mix_pre_g, (l2_s5_lambda_re, l2_s5_lambda_im, l2_s5_log_dt, l2_s5_b_re, l2_s5_b_im,
                        l2_s5_c_re, l2_s5_c_im, l2_s5_d, l2_s5_w_glu, l2_s5_b_glu), l2_mix_post_g,
         l2_ffn_pre_g, (l2_ffn_w_in, l2_ffn_conv_w, l2_ffn_conv_b, l2_ffn_w_out), l2_ffn_post_g),
        (l3_mix_pre_g, (l3_swa_w_qkv, l3_swa_b_qkv, l3_swa_sinks, l3_swa_w_o, l3_swa_b_o), l3_mix_post_g,
         l3_ffn_pre_g, (l3_ffn_w_in, l3_ffn_conv_w, l3_ffn_conv_b, l3_ffn_w_out), l3_ffn_post_g),
    )
    return _trunk(x, layers)
```

```python
import functools
import math

import jax
import jax.numpy as jnp
from jax import lax
from jax.experimental import pallas as pl
from jax.experimental.pallas import tpu as pltpu

F32 = jnp.float32
BF16 = jnp.bfloat16

NORM_EPS = 1e-6
MASK_VALUE = -1e30
LOG2E = math.log2(math.e)
ATTN_BLOCK = 128
WINDOW = 128
ROPE_THETA = 500000.0
SWA_KV_HEADS_PER_Q = 8
SWA_HEAD_DIM = 64
FOX_HEAD_DIM = 128
S5_GROUP = 16
S5_STATE = 64
S5_MAX_RE = -1e-4
CONV_WIDTH = 3

LANES = 128
SUBLANES = 8
VMEM_LIMIT_BYTES = 56 * 1024 * 1024


def _params(sem, vmem=VMEM_LIMIT_BYTES, fuse_inputs=None):
    return pltpu.CompilerParams(dimension_semantics=sem, vmem_limit_bytes=vmem, allow_input_fusion=fuse_inputs)


def _tile(dim, pref, mult):
    if dim <= pref:
        return dim
    t = (pref // mult) * mult
    while t > mult and dim % t:
        t -= mult
    assert dim % t == 0, (dim, pref, mult)
    return t


def _rms(x, g):
    ms = jnp.mean(x * x, axis=-1, keepdims=True)
    return x * lax.rsqrt(ms + NORM_EPS) * g


def _prenorm_kernel(x_ref, g_ref, o_ref):
    o_ref[...] = _rms(x_ref[...], g_ref[...]).astype(o_ref.dtype)


def _prenorm(x, g):
    m, d = x.shape
    tr = _tile(m, 256, SUBLANES)
    return pl.pallas_call(
        _prenorm_kernel,
        grid=(m // tr,),
        in_specs=[pl.BlockSpec((tr, d), lambda i: (i, 0)), pl.BlockSpec((1, d), lambda i: (0, 0))],
        out_specs=pl.BlockSpec((tr, d), lambda i: (i, 0)),
        out_shape=jax.ShapeDtypeStruct((m, d), BF16),
        compiler_params=_params(("parallel",)),
        name="prenorm",
    )(x, g.reshape(1, d))


def _post_pre_kernel(y_ref, h_ref, gp_ref, gn_ref, h_out_ref, hn_ref):
    h_new = h_ref[...] + _rms(y_ref[...].astype(F32), gp_ref[...])
    h_out_ref[...] = h_new
    hn_ref[...] = _rms(h_new, gn_ref[...]).astype(hn_ref.dtype)


def _post_kernel(y_ref, h_ref, gp_ref, h_out_ref):
    h_out_ref[...] = h_ref[...] + _rms(y_ref[...].astype(F32), gp_ref[...])


def _post_pre(y, h, g_post, g_next):
    m, d = h.shape
    tr = _tile(m, 256, SUBLANES)
    row = pl.BlockSpec((tr, d), lambda i: (i, 0))
    vec = pl.BlockSpec((1, d), lambda i: (0, 0))
    if g_next is None:
        return pl.pallas_call(
            _post_kernel, grid=(m // tr,), in_specs=[row, row, vec], out_specs=row,
            out_shape=jax.ShapeDtypeStruct((m, d), F32),
            compiler_params=_params(("parallel",)), name="post",
        )(y, h, g_post.reshape(1, d)), None
    return pl.pallas_call(
        _post_pre_kernel, grid=(m // tr,), in_specs=[row, row, vec, vec], out_specs=[row, row],
        out_shape=[jax.ShapeDtypeStruct((m, d), F32), jax.ShapeDtypeStruct((m, d), BF16)],
        compiler_params=_params(("parallel",)), name="post_pre",
    )(y, h, g_post.reshape(1, d), g_next.reshape(1, d))


def _mm_kernel(x_ref, w_ref, *rest, has_scale, has_bias):
    o_ref = rest[-1]
    acc = jnp.dot(x_ref[...], w_ref[...].astype(BF16), preferred_element_type=F32)
    if has_scale:
        acc = acc * rest[0][...]
    if has_bias:
        acc = acc + rest[-2][...]
    o_ref[...] = acc.astype(o_ref.dtype)


def _mm(x, w, b, out_dtype, tm_pref=1024, tn_pref=1024, name="mm", n=None, col_scale=None):
    m, k = x.shape
    n = w.shape[1] if n is None else n
    tm = _tile(m, tm_pref, 16)
    tn = _tile(n, tn_pref, LANES)
    vec = pl.BlockSpec((1, tn), lambda i, j: (0, j))
    in_specs = [pl.BlockSpec((tm, k), lambda i, j: (i, 0)), pl.BlockSpec((k, tn), lambda i, j: (0, j))]
    args = [x, w]
    for v in (col_scale, b):
        if v is not None:
            in_specs.append(vec)
            args.append(v.reshape(1, n).astype(F32))
    body = functools.partial(_mm_kernel, has_scale=col_scale is not None, has_bias=b is not None)
    return pl.pallas_call(
        body, grid=(m // tm, n // tn), in_specs=in_specs,
        out_specs=pl.BlockSpec((tm, tn), lambda i, j: (i, j)),
        out_shape=jax.ShapeDtypeStruct((m, n), out_dtype),
        compiler_params=_params(("parallel", "arbitrary"), fuse_inputs=[i == 1 for i in range(len(args))]),
        name=name,
    )(*args)


def _glu_kernel(x_ref, w1_ref, w2_ref, b1_ref, b2_ref, o_ref):
    x = x_ref[...]
    z1 = jnp.dot(x, w1_ref[...], preferred_element_type=F32) + b1_ref[...]
    z2 = jnp.dot(x, w2_ref[...], preferred_element_type=F32) + b2_ref[...]
    o_ref[...] = (z1 * jax.nn.sigmoid(z2)).astype(o_ref.dtype)


def _glu_mm(x, w, b, out_dtype):
    m, k = x.shape
    n = w.shape[1] // 2
    tm = _tile(m, 1024, 16)
    tn = _tile(n, 512, LANES)
    nj = n // tn
    b2 = b.reshape(1, 2 * n).astype(F32)
    return pl.pallas_call(
        _glu_kernel, grid=(m // tm, nj),
        in_specs=[pl.BlockSpec((tm, k), lambda i, j: (i, 0)),
                  pl.BlockSpec((k, tn), lambda i, j: (0, j)),
                  pl.BlockSpec((k, tn), lambda i, j: (0, j + nj)),
                  pl.BlockSpec((1, tn), lambda i, j: (0, j)),
                  pl.BlockSpec((1, tn), lambda i, j: (0, j + nj))],
        out_specs=pl.BlockSpec((tm, tn), lambda i, j: (i, j)),
        out_shape=jax.ShapeDtypeStruct((m, n), out_dtype),
        compiler_params=_params(("parallel", "arbitrary"), fuse_inputs=[False, True, True, False, False]),
        name="glu_mm",
    )(x, w, w, b2, b2)


def _ffn_in_kernel(x_ref, wg_ref, wu_ref, cwg_ref, cwu_ref, cbg_ref, cbu_ref, o_ref, carry_g, carry_u,
                   zg_ref, zu_ref, *, sub):
    i = pl.program_id(0)
    j = pl.program_id(1)
    tm = x_ref.shape[0]
    tn = o_ref.shape[1]
    s = SUBLANES
    nt = sub // s

    @pl.when(i == 0)
    def _():
        carry_g[j] = jnp.zeros(carry_g.shape[1:], F32)
        carry_u[j] = jnp.zeros(carry_u.shape[1:], F32)

    sidx = lax.broadcasted_iota(jnp.int32, (nt, s, tn), 1)

    def conv(z, prev, cw_ref, cb_ref):
        z3 = z.reshape(nt, s, tn)

        def shifted(d):
            r = pltpu.roll(z3, d, 1)
            rp = jnp.concatenate([pltpu.roll(prev, d, 0)[None], r[:-1]], axis=0)
            return jnp.where(sidx < d, rp, r)

        cw = cw_ref[...]
        out = cb_ref[...] + cw[2:3, :] * z3 + cw[0:1, :] * shifted(2) + cw[1:2, :] * shifted(1)
        return out.reshape(sub, tn), z3[nt - 1]

    wg = wg_ref[...].astype(BF16)
    wu = wu_ref[...].astype(BF16)
    prev_g = carry_g[j]
    prev_u = carry_u[j]
    zero = jnp.minimum(j, 0)

    def dots(r):
        x = x_ref[r * sub:(r + 1) * sub, :]
        zg_ref[zero + r % 2] = jnp.dot(x, wg, preferred_element_type=F32)
        zu_ref[zero + r % 2] = jnp.dot(x, wu, preferred_element_type=F32)

    nsub = tm // sub
    dots(0)
    for r in range(nsub):
        if r + 1 < nsub:
            dots(r + 1)
        gate, prev_g = conv(zg_ref[zero + r % 2], prev_g, cwg_ref, cbg_ref)
        up, prev_u = conv(zu_ref[zero + r % 2], prev_u, cwu_ref, cbu_ref)
        o_ref[r * sub:(r + 1) * sub, :] = (jax.nn.gelu(gate, approximate=True) * up).astype(o_ref.dtype)
    carry_g[j] = prev_g
    carry_u[j] = prev_u


def _ffn_in(x, w_in, conv_w, conv_b):
    m, k = x.shape
    dff = w_in.shape[1] // 2
    tm = _tile(m, 2048, 16)
    tn = _tile(dff, 256, LANES)
    nj = dff // tn
    cb = conv_b.reshape(1, 2 * dff)
    sub = _tile(tm, 128, 16)
    kern = functools.partial(_ffn_in_kernel, sub=sub)
    return pl.pallas_call(
        kern, grid=(m // tm, nj),
        in_specs=[pl.BlockSpec((tm, k), lambda i, j: (i, 0)),
                  pl.BlockSpec((k, tn), lambda i, j: (0, j)),
                  pl.BlockSpec((k, tn), lambda i, j: (0, j + nj)),
                  pl.BlockSpec((CONV_WIDTH, tn), lambda i, j: (0, j)),
                  pl.BlockSpec((CONV_WIDTH, tn), lambda i, j: (0, j + nj)),
                  pl.BlockSpec((1, tn), lambda i, j: (0, j)),
                  pl.BlockSpec((1, tn), lambda i, j: (0, j + nj))],
        out_specs=pl.BlockSpec((tm, tn), lambda i, j: (i, j)),
        out_shape=jax.ShapeDtypeStruct((m, dff), BF16),
        scratch_shapes=[pltpu.VMEM((nj, SUBLANES, tn), F32), pltpu.VMEM((nj, SUBLANES, tn), F32),
                        pltpu.VMEM((2, sub, tn), F32), pltpu.VMEM((2, sub, tn), F32)],
        compiler_params=_params(("arbitrary", "arbitrary")), name="ffn_in",
    )(x, w_in, w_in, conv_w, conv_w, cb, cb)


def _swa_kernel(sinks_ref, q_ref, ko_ref, kp_ref, vo_ref, vp_ref, ro_ref, rp_ref, bias_ref, o_ref, *,
                group, scale):
    n = pl.program_id(0)
    hp = pl.program_id(1)
    blk = q_ref.shape[0]
    half = LANES // 2
    nk = 2 * blk
    gw = group * half

    def rope(x, r):
        c, s1, s2 = r[:, 0:LANES], r[:, LANES:2 * LANES], r[:, 2 * LANES:3 * LANES]
        return x * c + pltpu.roll(x, LANES - 8, 1) * s1 + pltpu.roll(x, 8, 1) * s2

    r_own = ro_ref[...]
    r_both = jnp.concatenate([rp_ref[...], r_own], axis=0)
    lane = lax.broadcasted_iota(jnp.int32, (nk, LANES), 1)
    bias = bias_ref[jnp.minimum(n, 1)]
    kv_per_step = ko_ref.shape[1] // half

    def per_head(fn):
        return jnp.concatenate([jnp.broadcast_to(fn(g), (blk, nk)) for g in range(group)], axis=1)

    for kvi in range(kv_per_step):
        kv = kvi % 2
        if kv == 0:
            lanes = slice((kvi // 2) * LANES, (kvi // 2 + 1) * LANES)
            k = rope(jnp.concatenate([kp_ref[:, lanes], ko_ref[:, lanes]], axis=0).astype(F32), r_both)
            v = jnp.concatenate([vp_ref[:, lanes], vo_ref[:, lanes]], axis=0).astype(F32)
        own = (lane < half) if kv == 0 else (lane >= half)
        k_own = jnp.where(own, k, 0.0)
        v_own = jnp.where(own, v, 0.0)
        k_sw = pltpu.roll(k_own, half, 1)
        v_sw = pltpu.roll(v_own, half, 1)
        k_pad = [x.astype(BF16) for x in ((k_own, k_sw) if kv == 0 else (k_sw, k_own))]
        v_pad = [x.astype(BF16) for x in ((v_own, v_sw) if kv == 0 else (v_sw, v_own))]
        col0 = kvi * gw
        tiles = []
        for t in range(group // 2):
            q = rope(q_ref[:, col0 + t * LANES:col0 + (t + 1) * LANES].astype(F32), r_own)
            q = (q * (scale * LOG2E)).astype(BF16)
            for hh in range(2):
                tiles.append(lax.dot_general(q, k_pad[hh], (((1,), (1,)), ((), ())),
                                             preferred_element_type=F32))
        s = jnp.concatenate(tiles, axis=1) + bias
        sinks = [sinks_ref[(kv_per_step * hp + kvi) * group + g] * LOG2E for g in range(group)]
        m = [jnp.maximum(jnp.max(s[:, g * nk:(g + 1) * nk], axis=1, keepdims=True), sinks[g])
             for g in range(group)]
        p = jnp.exp2(s - per_head(lambda g: m[g]))
        inv = per_head(lambda g: 1.0 / (jnp.sum(p[:, g * nk:(g + 1) * nk], axis=1, keepdims=True)
                                        + jnp.exp2(sinks[g] - m[g])))
        p = (p * inv).astype(BF16)
        for t in range(group // 2):
            g0 = 2 * t
            o_tile = (jnp.dot(p[:, g0 * nk:(g0 + 1) * nk], v_pad[0], preferred_element_type=F32)
                      + jnp.dot(p[:, (g0 + 1) * nk:(g0 + 2) * nk], v_pad[1], preferred_element_type=F32))
            o_ref[:, col0 + t * LANES:col0 + (t + 1) * LANES] = o_tile.astype(o_ref.dtype)


def _rope_tables(l, head_dim):
    rope_dim = head_dim // 4
    hf = rope_dim // 2
    inv_freq = jnp.exp(-math.log(ROPE_THETA) * jnp.arange(hf, dtype=F32) * (2.0 / rope_dim))
    ang = jnp.arange(l).astype(F32)[:, None] * inv_freq[None, :]
    cos, sin = jnp.cos(ang), jnp.sin(ang)
    rest = head_dim - rope_dim
    c = jnp.concatenate([cos, cos, jnp.ones((l, rest), F32)], axis=1)
    s1 = jnp.concatenate([-sin, jnp.zeros((l, head_dim - hf), F32)], axis=1)
    s2 = jnp.concatenate([jnp.zeros((l, hf), F32), sin, jnp.zeros((l, rest), F32)], axis=1)
    reps = LANES // head_dim
    return jnp.concatenate([jnp.tile(c, (1, reps)), jnp.tile(s1, (1, reps)), jnp.tile(s2, (1, reps))], axis=1)


def _swa_attention(qkv, sinks, q_heads, kv_heads):
    l = qkv.shape[0]
    dh = SWA_HEAD_DIM
    group = q_heads // kv_heads
    qw = q_heads * dh
    kvw = kv_heads * dh
    assert dh * 2 == LANES and group % 2 == 0 and kv_heads % 2 == 0
    blk = ATTN_BLOCK
    nb = l // blk
    gw = group * dh
    kps = 2
    kw = kps * dh
    k_col0 = qw // kw
    v_col0 = (qw + kvw) // kw
    rope = _rope_tables(l, dh)
    nk = 2 * blk
    qpos = jnp.arange(blk)[:, None]
    kj = jnp.arange(group * nk)[None, :] % nk
    rel = qpos + blk - kj
    band = (rel >= 0) & (rel < WINDOW)
    bias = jnp.where(jnp.stack([band & (kj >= blk), band]), 0.0, MASK_VALUE).astype(F32)
    prev = lambda n: jnp.maximum(n - 1, 0)
    kern = functools.partial(_swa_kernel, group=group, scale=dh ** -0.5)
    return pl.pallas_call(
        kern, grid=(nb, kv_heads // kps),
        in_specs=[pl.BlockSpec(memory_space=pltpu.SMEM),
                  pl.BlockSpec((blk, kps * gw), lambda n, h: (n, h)),
                  pl.BlockSpec((blk, kw), lambda n, h: (n, k_col0 + h)),
                  pl.BlockSpec((blk, kw), lambda n, h: (prev(n), k_col0 + h)),
                  pl.BlockSpec((blk, kw), lambda n, h: (n, v_col0 + h)),
                  pl.BlockSpec((blk, kw), lambda n, h: (prev(n), v_col0 + h)),
                  pl.BlockSpec((blk, 3 * LANES), lambda n, h: (n, 0)),
                  pl.BlockSpec((blk, 3 * LANES), lambda n, h: (prev(n), 0)),
                  pl.BlockSpec((2, blk, group * nk), lambda n, h: (0, 0, 0))],
        out_specs=pl.BlockSpec((blk, kps * gw), lambda n, h: (n, h)),
        out_shape=jax.ShapeDtypeStruct((l, qw), BF16),
        compiler_params=_params(("parallel", "arbitrary")), name="swa_attn",
    )(sinks.astype(F32), qkv, qkv, qkv, qkv, qkv, rope, rope, bias)


def _logsig_cumsum_kernel(x_ref, b_ref, c_ref, hi_ref, mid_ref, lo_ref, carry_ref):
    i = pl.program_id(0)

    @pl.when(i == 0)
    def _():
        carry_ref[...] = jnp.zeros(carry_ref.shape, F32)

    x = x_ref[...] + b_ref[...]
    lf = jnp.minimum(x, 0.0) - jnp.log1p(jnp.exp(-jnp.abs(x)))
    rows = lf.shape[0]
    row = lax.broadcasted_iota(jnp.int32, lf.shape, 0)
    d = 1
    while d < rows:
        lf = lf + jnp.where(row >= d, pltpu.roll(lf, d, 0), 0.0)
        d *= 2
    lf = lf + carry_ref[0:1, :]
    carry_ref[...] = jnp.broadcast_to(lf[rows - 1:rows, :], carry_ref.shape)
    c = lf * LOG2E
    c_ref[...] = c
    neg = -c
    hi = neg.astype(BF16)
    rem = neg - hi.astype(F32)
    mid = rem.astype(BF16)
    hi_ref[...] = hi
    mid_ref[...] = mid
    lo_ref[...] = (rem - mid.astype(F32)).astype(BF16)


def _logsig_cumsum(x, b):
    l, w = x.shape
    tc = _tile(l, 512, 16)
    blk = pl.BlockSpec((tc, w), lambda i: (i, 0))
    return pl.pallas_call(
        _logsig_cumsum_kernel, grid=(l // tc,),
        in_specs=[blk, pl.BlockSpec((1, w), lambda i: (0, 0))],
        out_specs=[blk, blk, blk, blk],
        out_shape=[jax.ShapeDtypeStruct((l, w), F32)] + [jax.ShapeDtypeStruct((l, w), BF16)] * 3,
        scratch_shapes=[pltpu.VMEM((SUBLANES, w), F32)],
        compiler_params=_params(("arbitrary",)), name="logsig_cumsum",
    )(x, b)


def _fox_kernel(q_ref, k_ref, a_ref, v_ref, cq_ref, mask_ref, o_ref, m_ref, l_ref, acc_ref, t_ref, p_ref,
                tmax_ref, alpha_ref, *, tq, tk):
    h = pl.program_id(0)
    qi = pl.program_id(1)
    col = lax.broadcasted_iota(jnp.int32, (tq, LANES), 1)
    sel = jnp.where((col >= 3 * h) & (col < 3 * h + 3), 1.0, 0.0).astype(BF16)
    qa = jnp.concatenate([q_ref[...], sel], axis=1)
    cq = cq_ref[...]
    m_ref[...] = jnp.full(m_ref.shape, MASK_VALUE, F32)
    l_ref[...] = jnp.zeros(l_ref.shape, F32)
    acc_ref[...] = jnp.zeros(acc_ref.shape, F32)
    zero = jnp.minimum(qi, 0)
    p_ref[...] = jnp.zeros(p_ref.shape, BF16)
    alpha_ref[...] = jnp.ones(alpha_ref.shape, F32)

    def logits(j, slot):
        r0 = pl.multiple_of(j * tk, tk)
        ka = jnp.concatenate([k_ref[pl.ds(r0, tk), :], a_ref[pl.ds(r0, tk), :]], axis=1)
        t = lax.dot_general(ka, qa, (((1,), (1,)), ((), ())), preferred_element_type=F32)
        t_ref[zero + slot] = t
        tmax_ref[zero + slot] = jnp.max(t, axis=0, keepdims=True)

    def pv(j, slot):
        r0 = pl.multiple_of(jnp.maximum(j, 0) * tk, tk)
        return lax.dot_general(v_ref[pl.ds(r0, tk), :], p_ref[zero + slot], (((0,), (0,)), ((), ())),
                               preferred_element_type=F32)

    def softmax(j, slot, masked, pv_old, alpha_prev):
        t = t_ref[zero + slot]
        if masked:
            t = t + mask_ref[slot]
            tmax = jnp.max(t, axis=0, keepdims=True)
        else:
            tmax = tmax_ref[zero + slot]
        m_old = m_ref[...]
        m_new = jnp.maximum(m_old, tmax + cq)
        alpha = jnp.exp2(m_old - m_new)
        p = jnp.exp2(t - (m_new - cq))
        l_ref[...] = alpha * l_ref[...] + jnp.sum(p, axis=0, keepdims=True)
        p_ref[zero + slot] = p.astype(BF16)
        acc_ref[...] = alpha * (acc_ref[...] + alpha_prev * pv_old)
        m_ref[...] = m_new
        return alpha

    def pair(a, masked):
        pv0 = pv(a - 2, 0)
        pv1 = pv(a - 1, 1)
        alpha = softmax(a, 0, masked, pv0, alpha_ref[...])
        alpha = softmax(a + 1, 1, masked, pv1, alpha)
        alpha_ref[...] = alpha
        if not masked:
            logits(a + 2, 0)
            logits(a + 3, 1)

    assert tq == 2 * tk
    logits(0, 0)
    logits(1, 1)

    def full_body(it, c):
        pair(2 * it, False)
        return c

    lax.fori_loop(0, qi, full_body, 0)
    pair(2 * qi, True)
    acc = acc_ref[...] + alpha_ref[...] * pv(2 * qi, 0) + pv(2 * qi + 1, 1)
    o_ref[...] = (acc / l_ref[...]).T.astype(o_ref.dtype)


def _fox_attention(proj, c, pieces, heads):
    l = proj.shape[0]
    dh = FOX_HEAD_DIM
    hw = heads * dh
    assert dh == LANES and 3 * heads <= LANES
    tq = _tile(l, 1024, LANES)
    tk = _tile(tq, 512, LANES)
    a = jnp.stack([x[:, :heads] for x in pieces], axis=-1).reshape(l, 3 * heads)
    a = jnp.pad(a, ((0, 0), (0, LANES - 3 * heads)))
    cq = c[:, :heads].T.reshape(heads, 1, l)
    kr = jnp.arange(tk)[:, None]
    qc = jnp.arange(tq)[None, :]
    mask = jnp.where(jnp.stack([kr <= qc, tk + kr <= qc]), 0.0, MASK_VALUE).astype(F32)
    kern = functools.partial(_fox_kernel, tq=tq, tk=tk)
    return pl.pallas_call(
        kern, grid=(heads, l // tq),
        in_specs=[pl.BlockSpec((tq, dh), lambda h, i: (i, h)),
                  pl.BlockSpec((l, dh), lambda h, i: (0, heads + h)),
                  pl.BlockSpec((l, LANES), lambda h, i: (0, 0)),
                  pl.BlockSpec((l, dh), lambda h, i: (0, 2 * heads + h)),
                  pl.BlockSpec((None, 1, tq), lambda h, i: (h, 0, i)),
                  pl.BlockSpec((2, tk, tq), lambda h, i: (0, 0, 0))],
        out_specs=pl.BlockSpec((tq, dh), lambda h, i: (i, h)),
        out_shape=jax.ShapeDtypeStruct((l, hw), BF16),
        scratch_shapes=[pltpu.VMEM((1, tq), F32), pltpu.VMEM((1, tq), F32), pltpu.VMEM((dh, tq), F32),
                        pltpu.VMEM((2, tk, tq), F32), pltpu.VMEM((2, tk, tq), BF16),
                        pltpu.VMEM((2, 1, tq), F32), pltpu.VMEM((1, tq), F32)],
        compiler_params=_params(("parallel", "arbitrary")), name="fox_attn",
    )(proj, proj, a, proj, cq, mask)


S5_GROUPS_PER_BLOCK = 16


def _s5_kernel(u_ref, perm_ref, permt_ref, b_ref, tab_ref, ptab_ref, c_ref, d_ref, o_ref,
               up_ref, xs_ref, xl_ref, xb_ref, g_ref, carry_ref, *, ns, sub):
    c = pl.program_id(1)

    @pl.when(c == 0)
    def _():
        carry_ref[...] = jnp.zeros(carry_ref.shape, F32)

    t = u_ref.shape[0]
    s = SUBLANES
    nk = t // s
    nsub = t // sub
    kps = sub // s
    zero = jnp.minimum(c, 0)

    def tab(k):
        return tab_ref[k * s:(k + 1) * s, :]

    up_ref[...] = jnp.dot(perm_ref[...], u_ref[...], preferred_element_type=F32).astype(up_ref.dtype)
    gl = 2 * LANES
    groups = [(slice(g * gl, (g + 1) * gl), slice(ns + g * gl, ns + (g + 1) * gl)) for g in range(ns // gl)]

    def project_in(b):
        xs_ref[zero + b % 2] = jnp.dot(up_ref[b * sub:(b + 1) * sub, :], b_ref[...],
                                       preferred_element_type=F32)

    x_loc = [(jnp.zeros((s, gl), F32), jnp.zeros((s, gl), F32)) for _ in groups]
    project_in(0)
    for b in range(nsub):
        if b + 1 < nsub:
            project_in(b + 1)
        slot = zero + b % 2
        for g, (re, im) in enumerate(groups):
            lam_r = ptab_ref[0:s, re]
            lam_i = ptab_ref[nk * s:(nk + 1) * s, re]
            xr, xi = x_loc[g]
            for kk in range(kps):
                rows = slice((b * kps + kk) * s, (b * kps + kk + 1) * s)
                vr = xs_ref[slot, kk * s:(kk + 1) * s, re]
                vi = xs_ref[slot, kk * s:(kk + 1) * s, im]
                xr, xi = lam_r * xr - lam_i * xi + vr, lam_r * xi + lam_i * xr + vi
                xl_ref[rows, re] = xr
                xl_ref[rows, im] = xi
            x_loc[g] = (xr, xi)

    first = lax.broadcasted_iota(jnp.int32, (s, gl), 0) == 0
    enter = []
    for g, (re, im) in enumerate(groups):
        er, ei = x_loc[g]
        for idx, d in enumerate((1, 2, 4)):
            ar, ai = tab(2 * idx)[:, re], tab(2 * idx + 1)[:, re]
            sr, si = pltpu.roll(er, d, 0), pltpu.roll(ei, d, 0)
            er, ei = er + ar * sr - ai * si, ei + ar * si + ai * sr
        cr, ci = carry_ref[0:1, re], carry_ref[0:1, im]
        pr, pi = tab(6)[:, re], tab(7)[:, re]
        er, ei = er + pr * cr - pi * ci, ei + pr * ci + pi * cr
        carry_ref[0:1, re] = er[s - 1:s, :]
        carry_ref[0:1, im] = ei[s - 1:s, :]
        enter.append((jnp.where(first, cr, pltpu.roll(er, 1, 0)), jnp.where(first, ci, pltpu.roll(ei, 1, 0))))

    for b in range(nsub):
        slot = zero + b % 2
        for g, (re, im) in enumerate(groups):
            sr, si = enter[g]
            for kk in range(kps):
                k = b * kps + kk
                rows = slice(k * s, (k + 1) * s)
                pkr = ptab_ref[k * s:(k + 1) * s, re]
                pki = ptab_ref[(nk + k) * s:(nk + k + 1) * s, re]
                xb_ref[slot, kk * s:(kk + 1) * s, re] = xl_ref[rows, re] + pkr * sr - pki * si
                xb_ref[slot, kk * s:(kk + 1) * s, im] = xl_ref[rows, im] + pkr * si + pki * sr
        y = jnp.dot(xb_ref[slot].astype(BF16), c_ref[...], preferred_element_type=F32)
        y = y + d_ref[...] * up_ref[b * sub:(b + 1) * sub, :].astype(F32)
        g_ref[b * sub:(b + 1) * sub, :] = jax.nn.gelu(y, approximate=True).astype(g_ref.dtype)
    o_ref[...] = jnp.dot(permt_ref[...], g_ref[...], preferred_element_type=F32).astype(o_ref.dtype)


def _s5_tables(lam_re, lam_im, log_dt, b_re, b_im, c_re, c_im, nk):
    g, p = lam_re.shape
    hch = b_re.shape[2]
    gb = S5_GROUPS_PER_BLOCK
    nb = g // gb
    lre = jnp.minimum(lam_re.astype(F32), S5_MAX_RE)
    lim = lam_im.astype(F32)
    lam = lax.complex(lre, lim)
    dt = jnp.exp(log_dt.astype(F32))[:, None]
    lam_bar = jnp.exp(lam * dt)
    b_bar = ((lam_bar - 1.0) / lam)[..., None] * lax.complex(b_re.astype(F32), b_im.astype(F32))
    eye = jnp.eye(gb, dtype=F32)

    def in_map(x):
        return jnp.einsum('gaph,ab->gahbp', x.reshape(nb, gb, p, hch), eye).reshape(nb, gb * hch, gb * p)

    def out_map(x):
        return jnp.einsum('gahp,ab->gapbh', x.reshape(nb, gb, hch, p), eye).reshape(nb, gb * p, gb * hch)

    b_cat = jnp.concatenate([in_map(jnp.real(b_bar)), in_map(jnp.imag(b_bar))], axis=2).astype(BF16)
    c_cat = jnp.concatenate([out_map(c_re.astype(F32)), out_map(-c_im.astype(F32))], axis=1).astype(BF16)

    log_mag = (lre * dt).reshape(nb, 1, gb * p)
    phase = (lim * dt).reshape(nb, 1, gb * p)

    def powers(ks):
        k = jnp.asarray(list(ks), F32)[None, :, None]
        mag = jnp.exp(k * log_mag)
        return mag * jnp.cos(k * phase), mag * jnp.sin(k * phase)

    row = jnp.arange(SUBLANES)[None, :, None]
    parts = []
    for d in (1, 2, 4):
        re, im = powers([nk * d])
        parts += [jnp.where(row >= d, re, 0.0), jnp.where(row >= d, im, 0.0)]
    parts += list(powers([nk * (r + 1) for r in range(SUBLANES)]))
    tab = jnp.concatenate(parts, axis=1)
    ptab = jnp.concatenate(powers(range(1, nk + 1)), axis=1)
    return b_cat, c_cat, tab, jnp.repeat(ptab, SUBLANES, axis=1)


def _s5_core(hn, lam_re, lam_im, log_dt, b_re, b_im, c_re, c_im, d):
    l, dm = hn.shape
    gb = S5_GROUPS_PER_BLOCK
    wu = gb * S5_GROUP
    ns = gb * S5_STATE
    t = _tile(l, 512, 8 * SUBLANES)
    nk = t // SUBLANES
    sub = _tile(t, 128, 16)
    b_cat, c_cat, tab, ptab = _s5_tables(lam_re, lam_im, log_dt, b_re, b_im, c_re, c_im, nk)
    nb = b_cat.shape[0]
    p = jnp.arange(t)
    perm = (p[None, :] == ((p % SUBLANES) * nk + p // SUBLANES)[:, None]).astype(BF16)
    kern = functools.partial(_s5_kernel, ns=ns, sub=sub)
    whole = lambda g, c: (0, 0)
    return pl.pallas_call(
        kern, grid=(nb, l // t),
        in_specs=[pl.BlockSpec((t, wu), lambda g, c: (c, g)),
                  pl.BlockSpec((t, t), whole),
                  pl.BlockSpec((t, t), whole),
                  pl.BlockSpec((None, wu, 2 * ns), lambda g, c: (g, 0, 0)),
                  pl.BlockSpec((None, 8 * SUBLANES, ns), lambda g, c: (g, 0, 0)),
                  pl.BlockSpec((None, 2 * nk * SUBLANES, ns), lambda g, c: (g, 0, 0)),
                  pl.BlockSpec((None, 2 * ns, wu), lambda g, c: (g, 0, 0)),
                  pl.BlockSpec((1, wu), lambda g, c: (0, g))],
        out_specs=pl.BlockSpec((t, wu), lambda g, c: (c, g)),
        out_shape=jax.ShapeDtypeStruct((l, dm), BF16),
        scratch_shapes=[pltpu.VMEM((t, wu), BF16), pltpu.VMEM((2, sub, 2 * ns), F32),
                        pltpu.VMEM((t, 2 * ns), F32), pltpu.VMEM((2, sub, 2 * ns), F32),
                        pltpu.VMEM((t, wu), BF16), pltpu.VMEM((SUBLANES, 2 * ns), F32)],
        compiler_params=_params(("parallel", "arbitrary")), name="s5_core",
    )(hn, perm, perm.T, b_cat, tab, ptab, c_cat, d.reshape(1, dm).astype(F32))


def _swa_mixer(hn, w_qkv, b_qkv, sinks, w_o, b_o):
    q_heads = sinks.shape[0]
    kv_heads = q_heads // SWA_KV_HEADS_PER_Q
    qkv = _mm(hn, w_qkv, b_qkv, BF16, tn_pref=512, name="swa_qkv")
    o = _swa_attention(qkv, sinks, q_heads, kv_heads)
    return _mm(o, w_o.astype(BF16), b_o, BF16, name="swa_out")


def _fox_mixer(hn, w_qkvf, b_f, w_o):
    heads = b_f.shape[0]
    hw = heads * FOX_HEAD_DIM
    q_scale = jnp.concatenate([jnp.full((hw,), FOX_HEAD_DIM ** -0.5 * LOG2E, F32), jnp.ones((2 * hw,), F32)])
    proj = _mm(hn, w_qkvf, None, BF16, tn_pref=512, name="fox_qkv", n=3 * hw, col_scale=q_scale)
    pad = LANES - heads
    w_f = jnp.pad(w_qkvf[:, 3 * hw:], ((0, 0), (0, pad))).astype(BF16)
    logits = _mm(hn, w_f, None, F32, name="fox_gate")
    c, hi, mid, lo = _logsig_cumsum(logits, jnp.pad(b_f.astype(F32), (0, pad)).reshape(1, LANES))
    o = _fox_attention(proj, c, (hi, mid, lo), heads)
    return _mm(o, w_o.astype(BF16), None, BF16, name="fox_out")


def _s5_mixer(hn, lam_re, lam_im, log_dt, b_re, b_im, c_re, c_im, d, w_glu, b_glu):
    g = _s5_core(hn, lam_re, lam_im, log_dt, b_re, b_im, c_re, c_im, d)
    return _glu_mm(g, w_glu.astype(BF16), b_glu, BF16)


def _conv_ffn(hn, w_in, conv_w, conv_b, w_out):
    g = _ffn_in(hn, w_in, conv_w.astype(F32), conv_b.astype(F32))
    return _mm(g, w_out.astype(BF16), None, BF16, tm_pref=512, tn_pref=512, name="ffn_out")


def _trunk(x, layers):
    b, l, d = x.shape
    assert b == 1
    h = x.reshape(l, d)
    mixers = (_swa_mixer, _fox_mixer, _s5_mixer)
    hn = _prenorm(h, layers[0][0])
    for i, (mix_pre, mix_params, mix_post, ffn_pre, ffn_params, ffn_post) in enumerate(layers):
        y = mixers[i % len(mixers)](hn, *mix_params)
        h, hn = _post_pre(y, h, mix_post, ffn_pre)
        y = _conv_ffn(hn, *ffn_params)
        nxt = layers[i + 1][0] if i + 1 < len(layers) else None
        h, hn = _post_pre(y, h, ffn_post, nxt)
    return h.reshape(b, l, d)


def kernel(x,
           l0_mix_pre_g, l0_swa_w_qkv, l0_swa_b_qkv, l0_swa_sinks, l0_swa_w_o, l0_swa_b_o, l0_mix_post_g,
           l0_ffn_pre_g, l0_ffn_w_in, l0_ffn_conv_w, l0_ffn_conv_b, l0_ffn_w_out, l0_ffn_post_g,
           l1_mix_pre_g, l1_fox_w_qkvf, l1_fox_b_f, l1_fox_w_o, l1_mix_post_g,
           l1_ffn_pre_g, l1_ffn_w_in, l1_ffn_conv_w, l1_ffn_conv_b, l1_ffn_w_out, l1_ffn_post_g,
           l2_mix_pre_g, l2_s5_lambda_re, l2_s5_lambda_im, l2_s5_log_dt, l2_s5_b_re, l2_s5_b_im,
           l2_s5_c_re, l2_s5_c_im, l2_s5_d, l2_s5_w_glu, l2_s5_b_glu, l2_mix_post_g,
           l2_ffn_pre_g, l2_ffn_w_in, l2_ffn_conv_w, l2_ffn_conv_b, l2_ffn_w_out, l2_ffn_post_g,
           l3_mix_pre_g, l3_swa_w_qkv, l3_swa_b_qkv, l3_swa_sinks, l3_swa_w_o, l3_swa_b_o, l3_mix_post_g,
           l3_ffn_pre_g, l3_ffn_w_in, l3_ffn_conv_w, l3_ffn_conv_b, l3_ffn_w_out, l3_ffn_post_g):
    layers = (
        (l0_mix_pre_g, (l0_swa_w_qkv, l0_swa_b_qkv, l0_swa_sinks, l0_swa_w_o, l0_swa_b_o), l0_mix_post_g,
         l0_ffn_pre_g, (l0_ffn_w_in, l0_ffn_conv_w, l0_ffn_conv_b, l0_ffn_w_out), l0_ffn_post_g),
        (l1_mix_pre_g, (l1_fox_w_qkvf, l1_fox_b_f, l1_fox_w_o), l1_mix_post_g,
         l1_ffn_pre_g, (l1_ffn_w_in, l1_ffn_conv_w, l1_ffn_conv_b, l1_ffn_w_out), l1_ffn_post_g),
        (l2_mix_pre_g, (l2_s5_lambda_re, l2_s5_lambda_im, l2_s5_log_dt, l2_s5_b_re, l2_s5_b_im,
                        l2_s5_c_re, l2_s5_c_im, l2_s5_d, l2_s5_w_glu, l2_s5_b_glu), l2_mix_post_g,
         l2_ffn_pre_g, (l2_ffn_w_in, l2_ffn_conv_w, l2_ffn_conv_b, l2_ffn_w_out), l2_ffn_post_g),
        (l3_mix_pre_g, (l3_swa_w_qkv, l3_swa_b_qkv, l3_swa_sinks, l3_swa_w_o, l3_swa_b_o), l3_mix_post_g,
         l3_ffn_pre_g, (l3_ffn_w_in, l3_ffn_conv_w, l3_ffn_conv_b, l3_ffn_w_out), l3_ffn_post_g),
    )
    return _trunk(x, layers)
```
